```python
import math
import jax, jax.numpy as jnp
from jax import lax
import numpy as np

D_MODEL = 2048
BATCH = 16
SEQ = 2048
DEPTH = 4
DEC_BATCH = 8
DEC_SEQ = 4096
PAST_LEN = 128

N_MIXERS = 4
LN_EPS = 1e-5
DEEPNORM_ALPHA = (2 * DEPTH) ** 0.25
DEEPNORM_BETA = (8 * DEPTH) ** -0.25

CONV_WIDTH = 31

DIFF_HEADS = D_MODEL // 256
DIFF_HEAD_DIM = 128
DIFF_Q_BLOCK = 128
SUBLN_EPS = 1e-5

LRU_WIDTH = D_MODEL
LRU_BLOCK_W = 256
LRU_BLOCKS = LRU_WIDTH // LRU_BLOCK_W
LRU_CONV = 4
LRU_C = 8.0

SWA_HEADS = D_MODEL // 128
SWA_KV_HEADS = 4
SWA_HEAD_DIM = 128
SWA_WINDOW = 128
SWA_BLOCK = 128

FFN_DIM = 2 * D_MODEL
N_EXPERTS = 8
TOP_K = 2
EXPERT_DIM = D_MODEL // 2

kernel_name = 'hybrid_bidir_encoder_interleaved'

F32 = jnp.float32


def layer_norm(x, g, b):
    xf = x.astype(F32)
    mu = jnp.mean(xf, -1, keepdims=True)
    var = jnp.mean(jnp.square(xf - mu), -1, keepdims=True)
    y = (xf - mu) * lax.rsqrt(var + LN_EPS) * g.astype(F32) + b.astype(F32)
    return y.astype(x.dtype)


def alibi_slopes(n_heads):
    return 2.0 ** (-8.0 * jnp.arange(1, n_heads + 1, dtype=F32) / n_heads)


def depthwise_conv(x, w, b, pad_left, pad_right):
    y = lax.conv_general_dilated(
        x, w[:, None, :].astype(x.dtype), window_strides=(1,),
        padding=[(pad_left, pad_right)], dimension_numbers=('NWC', 'WIO', 'NWC'),
        feature_group_count=x.shape[-1])
    return y + b


def conformer_conv(x, w_in, b_in, w_dw, b_dw, norm_g, norm_b, w_out, b_out):
    h = x @ w_in + b_in
    a, g = jnp.split(h, 2, axis=-1)
    h = a * jax.nn.sigmoid(g)
    half = (CONV_WIDTH - 1) // 2
    h = depthwise_conv(h, w_dw, b_dw, half, half)
    h = jax.nn.silu(layer_norm(h, norm_g, norm_b))
    return h @ w_out + b_out


def diff_lambda_init(layer_idx):
    return 0.8 - 0.6 * math.exp(-0.3 * layer_idx)


def diff_attention(x, w_qkv, lam, subln_g, w_out, lambda_init):
    B, S, _ = x.shape
    H, d = DIFF_HEADS, DIFF_HEAD_DIM
    q, k, v = jnp.split(x @ w_qkv, 3, axis=-1)
    k = k.reshape(B, S, H, 2, d)
    v = v.reshape(B, S, H, 2 * d)
    nq = S // DIFF_Q_BLOCK
    qb = q.reshape(B, nq, DIFF_Q_BLOCK, H, 2, d).transpose(1, 0, 2, 3, 4, 5)
    lf = lam.astype(F32)
    lam_full = jnp.exp(jnp.sum(lf[0] * lf[1])) - jnp.exp(jnp.sum(lf[2] * lf[3])) + lambda_init
    slopes = alibi_slopes(H)[None, :, None, None, None]
    kpos = jnp.arange(S)
    scale = d ** -0.5

    def block(args):
        qi, j = args
        s = jnp.einsum('bqhcd,bkhcd->bhcqk', qi, k, preferred_element_type=F32) * scale
        qpos = j * DIFF_Q_BLOCK + jnp.arange(DIFF_Q_BLOCK)
        dist = jnp.abs(qpos[:, None] - kpos[None, :]).astype(F32)
        p = jax.nn.softmax(s - slopes * dist, axis=-1)
        attn = p[:, :, 0] - lam_full * p[:, :, 1]
        return jnp.einsum('bhqk,bkhe->bqhe', attn.astype(v.dtype), v)

    o = lax.map(block, (qb, jnp.arange(nq)))
    o = o.transpose(1, 0, 2, 3, 4).reshape(B, S, H, 2 * d).astype(F32)
    o = o * lax.rsqrt(jnp.mean(o * o, -1, keepdims=True) + SUBLN_EPS) * subln_g.astype(F32)
    o = (o * (1.0 - lambda_init)).reshape(B, S, H * 2 * d).astype(x.dtype)
    return o @ w_out


def rglru_direction(xf, gate_w, gate_b, lam, reverse):
    B, S, _ = xf.shape
    xb = xf.reshape(B, S, LRU_BLOCKS, LRU_BLOCK_W)
    gates = jnp.einsum('bsni,gnio->gbsno', xb, gate_w.astype(F32)).reshape(2, B, S, LRU_WIDTH)
    gates = gates + gate_b.astype(F32)[:, None, None, :]
    r = jax.nn.sigmoid(gates[0])
    i = jax.nn.sigmoid(gates[1])
    log_a = -LRU_C * r * jax.nn.softplus(-lam.astype(F32))
    a = jnp.exp(log_a)
    b = jnp.sqrt(-jnp.expm1(2.0 * log_a)) * (i * xf)

    def step(h, ab):
        a_t, b_t = ab
        h = a_t * h + b_t
        return h, h

    _, hs = lax.scan(step, jnp.zeros((B, LRU_WIDTH), F32),
                     (a.transpose(1, 0, 2), b.transpose(1, 0, 2)), reverse=reverse)
    return hs.transpose(1, 0, 2)


def recurrent_block(x, w_in, b_in, conv_w, conv_b, gate_w, gate_b, lam, w_out, b_out):
    h = x @ w_in + b_in
    y_branch, r_branch = jnp.split(h, 2, axis=-1)
    y_branch = jax.nn.gelu(y_branch)
    left = LRU_CONV // 2
    r_branch = depthwise_conv(r_branch, conv_w, conv_b, left, LRU_CONV - 1 - left)
    rf = r_branch.astype(F32)
    hr = (rglru_direction(rf, gate_w[0], gate_b[0], lam[0], False)
          + rglru_direction(rf, gate_w[1], gate_b[1], lam[1], True))
    return (hr.astype(x.dtype) * y_branch) @ w_out + b_out


def window_attention(x, w_qkv, sink, w_out):
    B, S, _ = x.shape
    H, KV, d = SWA_HEADS, SWA_KV_HEADS, SWA_HEAD_DIM
    G = H // KV
    nb = S // SWA_BLOCK
    qkv = x @ w_qkv
    q = qkv[..., :H * d].reshape(B, nb, SWA_BLOCK, KV, G, d)
    k = qkv[..., H * d:(H + KV) * d].reshape(B, S, KV, d)
    v = qkv[..., (H + KV) * d:].reshape(B, S, KV, d)
    pad = ((0, 0), (SWA_BLOCK, SWA_BLOCK), (0, 0), (0, 0))
    kp = jnp.pad(k, pad).reshape(B, nb + 2, SWA_BLOCK, KV, d)
    vp = jnp.pad(v, pad).reshape(B, nb + 2, SWA_BLOCK, KV, d)
    kw = jnp.concatenate([kp[:, :-2], kp[:, 1:-1], kp[:, 2:]], axis=2)
    vw = jnp.concatenate([vp[:, :-2], vp[:, 1:-1], vp[:, 2:]], axis=2)
    s = jnp.einsum('bnqhgd,bnchd->bnhgqc', q, kw, preferred_element_type=F32) * (d ** -0.5)
    blk = jnp.arange(nb)[:, None] * SWA_BLOCK
    qpos = blk + jnp.arange(SWA_BLOCK)[None, :]
    kpos = blk - SWA_BLOCK + jnp.arange(3 * SWA_BLOCK)[None, :]
    rel = jnp.abs(qpos[:, :, None] - kpos[:, None, :])
    valid = (rel <= SWA_WINDOW) & (kpos[:, None, :] >= 0) & (kpos[:, None, :] < S)
    slopes = alibi_slopes(H).reshape(KV, G)[None, None, :, :, None, None]
    s = s - slopes * rel.astype(F32)[None, :, None, None]
    s = jnp.where(valid[None, :, None, None], s, -jnp.inf)
    sink_b = sink.astype(F32).reshape(KV, G)[None, None, :, :, None, None]
    m = jnp.maximum(jnp.max(s, -1, keepdims=True), sink_b)
    p = jnp.exp(s - m)
    p = p / (jnp.sum(p, -1, keepdims=True) + jnp.exp(sink_b - m))
    o = jnp.einsum('bnhgqc,bnchd->bnqhgd', p.astype(vw.dtype), vw)
    return o.reshape(B, S, H * d) @ w_out


def swiglu(x, w_gate, w_up, w_down):
    return (jax.nn.silu(x @ w_gate) * (x @ w_up)) @ w_down


def moe_swiglu(x, w_router, w_gate, w_up, w_down):
    B, S, D = x.shape
    t = x.reshape(B * S, D)
    logits = (t @ w_router).astype(F32)
    top_v, top_i = lax.top_k(logits, TOP_K)
    top_w = jax.nn.softmax(top_v, axis=-1)
    gates = jnp.sum(jax.nn.one_hot(top_i, N_EXPERTS, dtype=F32) * top_w[..., None], axis=1)
    gates = gates.astype(t.dtype)
    g = jnp.einsum('td,edf->tef', t, w_gate)
    u = jnp.einsum('td,edf->tef', t, w_up)
    h = jax.nn.silu(g) * u * gates[:, :, None]
    out = jnp.einsum('tef,efd->td', h, w_down)
    return out.reshape(B, S, D)


def setup_inputs(seed: int = 0) -> dict:
    key = jax.random.key(seed)
    ks = iter(jax.random.split(key, 96))
    D = D_MODEL
    beta = DEEPNORM_BETA

    def w(shape, fan_in, scale=1.0):
        return jax.random.normal(next(ks), shape, F32) * (scale * fan_in ** -0.5)

    def gain(shape):
        return 1.0 + 0.02 * jax.random.normal(next(ks), shape, F32)

    def bias(shape):
        return 0.02 * jax.random.normal(next(ks), shape, F32)

    inp = {}
    inp['x_prompt'] = jax.random.normal(next(ks), (BATCH, SEQ, D), F32)
    inp['x_sample'] = jax.random.normal(next(ks), (DEC_BATCH, DEC_SEQ, D), F32)
    inp['l0_conv_w_in'] = w((D, 2 * D), D)
    inp['l0_conv_b_in'] = bias((2 * D,))
    inp['l0_conv_w_dw'] = w((CONV_WIDTH, D), CONV_WIDTH)
    inp['l0_conv_b_dw'] = bias((D,))
    inp['l0_conv_norm_g'] = gain((D,))
    inp['l0_conv_norm_b'] = bias((D,))
    inp['l0_conv_w_out'] = w((D, D), D, beta)
    inp['l0_conv_b_out'] = bias((D,))
    inp['l0_ln1_g'] = gain((D,))
    inp['l0_ln1_b'] = bias((D,))
    inp['l0_ffn_w_gate'] = w((D, FFN_DIM), D)
    inp['l0_ffn_w_up'] = w((D, FFN_DIM), D)
    inp['l0_ffn_w_down'] = w((FFN_DIM, D), FFN_DIM, beta)
    inp['l0_ln2_g'] = gain((D,))
    inp['l0_ln2_b'] = bias((D,))
    diff_w = DIFF_HEADS * 2 * DIFF_HEAD_DIM
    inp['l1_attn_w_qkv'] = w((D, 3 * diff_w), D)
    inp['l1_attn_lambda'] = 0.1 * jax.random.normal(next(ks), (4, DIFF_HEAD_DIM), F32)
    inp['l1_attn_subln_g'] = gain((2 * DIFF_HEAD_DIM,))
    inp['l1_attn_w_out'] = w((diff_w, D), diff_w, beta)
    inp['l1_ln1_g'] = gain((D,))
    inp['l1_ln1_b'] = bias((D,))
    inp['l1_moe_w_router'] = w((D, N_EXPERTS), D)
    inp['l1_moe_w_gate'] = w((N_EXPERTS, D, EXPERT_DIM), D)
    inp['l1_moe_w_up'] = w((N_EXPERTS, D, EXPERT_DIM), D)
    inp['l1_moe_w_down'] = w((N_EXPERTS, EXPERT_DIM, D), EXPERT_DIM, beta)
    inp['l1_ln2_g'] = gain((D,))
    inp['l1_ln2_b'] = bias((D,))
    inp['l2_rec_w_in'] = w((D, 2 * LRU_WIDTH), D)
    inp['l2_rec_b_in'] = bias((2 * LRU_WIDTH,))
    inp['l2_rec_conv_w'] = w((LRU_CONV, LRU_WIDTH), LRU_CONV)
    inp['l2_rec_conv_b'] = bias((LRU_WIDTH,))
    inp['l2_rec_gate_w'] = w((2, 2, LRU_BLOCKS, LRU_BLOCK_W, LRU_BLOCK_W), LRU_BLOCK_W)
    inp['l2_rec_gate_b'] = bias((2, 2, LRU_WIDTH))
    u = jax.random.uniform(next(ks), (2, LRU_WIDTH), F32, 0.9, 0.999)
    a_base = u ** (1.0 / LRU_C)
    inp['l2_rec_lambda'] = jnp.log(a_base) - jnp.log1p(-a_base)
    inp['l2_rec_w_out'] = w((LRU_WIDTH, D), LRU_WIDTH, beta)
    inp['l2_rec_b_out'] = bias((D,))
    inp['l2_ln1_g'] = gain((D,))
    inp['l2_ln1_b'] = bias((D,))
    inp['l2_ffn_w_gate'] = w((D, FFN_DIM), D)
    inp['l2_ffn_w_up'] = w((D, FFN_DIM), D)
    inp['l2_ffn_w_down'] = w((FFN_DIM, D), FFN_DIM, beta)
    inp['l2_ln2_g'] = gain((D,))
    inp['l2_ln2_b'] = bias((D,))
    swa_q = SWA_HEADS * SWA_HEAD_DIM
    inp['l3_attn_w_qkv'] = w((D, swa_q + 2 * SWA_KV_HEADS * SWA_HEAD_DIM), D)
    inp['l3_attn_sink'] = jax.random.normal(next(ks), (SWA_HEADS,), F32)
    inp['l3_attn_w_out'] = w((swa_q, D), swa_q, beta)
    inp['l3_ln1_g'] = gain((D,))
    inp['l3_ln1_b'] = bias((D,))
    inp['l3_moe_w_router'] = w((D, N_EXPERTS), D)
    inp['l3_moe_w_gate'] = w((N_EXPERTS, D, EXPERT_DIM), D)
    inp['l3_moe_w_up'] = w((N_EXPERTS, D, EXPERT_DIM), D)
    inp['l3_moe_w_down'] = w((N_EXPERTS, EXPERT_DIM, D), EXPERT_DIM, beta)
    inp['l3_ln2_g'] = gain((D,))
    inp['l3_ln2_b'] = bias((D,))
    return inp


def reference(x_prompt, x_sample,
              l0_conv_w_in, l0_conv_b_in, l0_conv_w_dw, l0_conv_b_dw, l0_conv_norm_g, l0_conv_norm_b,
              l0_conv_w_out, l0_conv_b_out, l0_ln1_g, l0_ln1_b, l0_ffn_w_gate, l0_ffn_w_up, l0_ffn_w_down,
              l0_ln2_g, l0_ln2_b,
              l1_attn_w_qkv, l1_attn_lambda, l1_attn_subln_g, l1_attn_w_out, l1_ln1_g, l1_ln1_b,
              l1_moe_w_router, l1_moe_w_gate, l1_moe_w_up, l1_moe_w_down, l1_ln2_g, l1_ln2_b,
              l2_rec_w_in, l2_rec_b_in, l2_rec_conv_w, l2_rec_conv_b, l2_rec_gate_w, l2_rec_gate_b,
              l2_rec_lambda, l2_rec_w_out, l2_rec_b_out, l2_ln1_g, l2_ln1_b, l2_ffn_w_gate, l2_ffn_w_up,
              l2_ffn_w_down, l2_ln2_g, l2_ln2_b,
              l3_attn_w_qkv, l3_attn_sink, l3_attn_w_out, l3_ln1_g, l3_ln1_b,
              l3_moe_w_router, l3_moe_w_gate, l3_moe_w_up, l3_moe_w_down, l3_ln2_g, l3_ln2_b):
    layers = [
        ((l0_conv_w_in, l0_conv_b_in, l0_conv_w_dw, l0_conv_b_dw, l0_conv_norm_g, l0_conv_norm_b,
          l0_conv_w_out, l0_conv_b_out), (l0_ln1_g, l0_ln1_b),
         (l0_ffn_w_gate, l0_ffn_w_up, l0_ffn_w_down), (l0_ln2_g, l0_ln2_b)),
        ((l1_attn_w_qkv, l1_attn_lambda, l1_attn_subln_g, l1_attn_w_out), (l1_ln1_g, l1_ln1_b),
         (l1_moe_w_router, l1_moe_w_gate, l1_moe_w_up, l1_moe_w_down), (l1_ln2_g, l1_ln2_b)),
        ((l2_rec_w_in, l2_rec_b_in, l2_rec_conv_w, l2_rec_conv_b, l2_rec_gate_w, l2_rec_gate_b,
          l2_rec_lambda, l2_rec_w_out, l2_rec_b_out), (l2_ln1_g, l2_ln1_b),
         (l2_ffn_w_gate, l2_ffn_w_up, l2_ffn_w_down), (l2_ln2_g, l2_ln2_b)),
        ((l3_attn_w_qkv, l3_attn_sink, l3_attn_w_out), (l3_ln1_g, l3_ln1_b),
         (l3_moe_w_router, l3_moe_w_gate, l3_moe_w_up, l3_moe_w_down), (l3_ln2_g, l3_ln2_b)),
    ]

    def trunk(x):
        for i in range(DEPTH):
            mixer_p, ln1, ffn_p, ln2 = layers[i]
            kind = i % N_MIXERS
            if kind == 0:
                h = conformer_conv(x, *mixer_p)
            elif kind == 1:
                h = diff_attention(x, *mixer_p, lambda_init=diff_lambda_init(i))
            elif kind == 2:
                h = recurrent_block(x, *mixer_p)
            else:
                h = window_attention(x, *mixer_p)
            x = layer_norm(DEEPNORM_ALPHA * x + h, *ln1)
            f = swiglu(x, *ffn_p) if i % 2 == 0 else moe_swiglu(x, *ffn_p)
            x = layer_norm(DEEPNORM_ALPHA * x + f, *ln2)
        return x

    y_prompt = trunk(x_prompt)
    y_sample = trunk(x_sample)
    return (y_prompt, y_sample)
```

```python
import functools
import math

import jax
import jax.numpy as jnp
from jax import lax
from jax.experimental import pallas as pl
from jax.experimental.pallas import tpu as pltpu

F32 = jnp.float32
BF16 = jnp.bfloat16
I32 = jnp.int32

N_LAYERS = 4
LN_EPS = 1e-5
SUBLN_EPS = 1e-5
ALPHA = (2 * N_LAYERS) ** 0.25
LRU_C = 8.0
DIFF_HEAD_DIM = 128
SWA_HEAD_DIM = 128
SWA_KV_HEADS = 4
SWA_BLOCK = 128
TOP_K = 2
LANES = 128
HALO_ROWS = 16
LRU_HALO = 8
MIB = 1024 * 1024


def _cparams(vmem_mib, n_axes):
    return pltpu.CompilerParams(
        dimension_semantics=("arbitrary",) * n_axes, vmem_limit_bytes=vmem_mib * MIB)


def _tile(n, pref):
    t = min(n, pref)
    while n % t:
        t //= 2
    return t


def _sigmoid(x):
    return 0.5 * jnp.tanh(0.5 * x) + 0.5


def _silu(x):
    return x * _sigmoid(x)


def _gelu_tanh(x):
    return 0.5 * x * (1.0 + jnp.tanh(math.sqrt(2.0 / math.pi) * (x + 0.044715 * (x * x * x))))


def _dot(a, b):
    return jnp.dot(a, b, preferred_element_type=F32)


def _dot_nt(a, b):
    return lax.dot_general(a, b, (((1,), (1,)), ((), ())), preferred_element_type=F32)


def _layer_norm(y, g, b):
    mu = jnp.mean(y, axis=-1, keepdims=True)
    yc = y - mu
    var = jnp.mean(yc * yc, axis=-1, keepdims=True)
    return yc * lax.rsqrt(var + LN_EPS) * g + b


def _mm_body(*refs, has_bias, act):
    if has_bias:
        x_ref, w_ref, b_ref, o_ref = refs
    else:
        x_ref, w_ref, o_ref = refs
    acc = _dot(x_ref[...].astype(BF16), w_ref[...])
    if has_bias:
        acc = acc + b_ref[...]
    if act == "gelu":
        acc = _gelu_tanh(acc)
    o_ref[...] = acc.astype(o_ref.dtype)


def _mm(x, w, bias, *, n_out, col_off=0, act=None, out_dtype=F32):
    m, k = x.shape
    tm, tn = _tile(m, 512), _tile(n_out, 512)
    joff = col_off // tn
    in_specs = [pl.BlockSpec((tm, k), lambda i, j: (i, 0)),
                pl.BlockSpec((k, tn), lambda i, j: (0, j + joff))]
    args = [x, w]
    if bias is not None:
        in_specs.append(pl.BlockSpec((1, tn), lambda i, j: (0, j + joff)))
        args.append(bias)
    return pl.pallas_call(
        functools.partial(_mm_body, has_bias=bias is not None, act=act),
        grid=(m // tm, n_out // tn),
        in_specs=in_specs,
        out_specs=pl.BlockSpec((tm, tn), lambda i, j: (i, j)),
        out_shape=jax.ShapeDtypeStruct((m, n_out), out_dtype),
        compiler_params=_cparams(40, 2),
        name="mm",
    )(*args)


def _mm_glu_body(x_ref, wa_ref, wg_ref, ba_ref, bg_ref, o_ref):
    xb = x_ref[...].astype(BF16)
    a = _dot(xb, wa_ref[...]) + ba_ref[...]
    g = _dot(xb, wg_ref[...]) + bg_ref[...]
    o_ref[...] = a * _sigmoid(g)


def _mm_glu(x, w, bias):
    m, k = x.shape
    n = w.shape[1] // 2
    tm, tn = _tile(m, 512), _tile(n, 512)
    nj = n // tn
    return pl.pallas_call(
        _mm_glu_body,
        grid=(m // tm, nj),
        in_specs=[pl.BlockSpec((tm, k), lambda i, j: (i, 0)),
                  pl.BlockSpec((k, tn), lambda i, j: (0, j)),
                  pl.BlockSpec((k, tn), lambda i, j: (0, j + nj)),
                  pl.BlockSpec((1, tn), lambda i, j: (0, j)),
                  pl.BlockSpec((1, tn), lambda i, j: (0, j + nj))],
        out_specs=pl.BlockSpec((tm, tn), lambda i, j: (i, j)),
        out_shape=jax.ShapeDtypeStruct((m, n), F32),
        compiler_params=_cparams(40, 2),
        name="mm_glu",
    )(x, w, w, bias, bias)


def _conv_out_body(cur_ref, prv_ref, nxt_ref, res_ref, wdw_ref, bdw_ref, ng_ref, nb_ref,
                   w_ref, bo_ref, g_ref, b_ref, o_ref, ext, cslab, lhs, *, seq, ts, width):
    i = pl.program_id(0)
    t0 = i * ts
    first = (t0 % seq) == 0
    last = ((t0 + ts) % seq) == 0
    nsl = cur_ref.shape[1] // LANES
    half = (width - 1) // 2
    for c in range(nsl):
        cs = slice(c * LANES, (c + 1) * LANES)
        ext[c, 0:HALO_ROWS, :] = jnp.where(first, 0.0, prv_ref[:, cs])
        ext[c, HALO_ROWS:HALO_ROWS + ts, :] = cur_ref[:, cs]
        ext[c, HALO_ROWS + ts:, :] = jnp.where(last, 0.0, nxt_ref[:, cs])

    def slab(c, carry):
        wk = wdw_ref[c]
        acc = jnp.broadcast_to(bdw_ref[c], (ts, LANES))
        for k in range(width):
            acc = acc + wk[k:k + 1, :] * ext[c, pl.ds(HALO_ROWS - half + k, ts), :]
        cslab[c] = acc
        return carry

    lax.fori_loop(0, nsl, slab, 0)

    s1 = cslab[0]
    for c in range(1, nsl):
        s1 = s1 + cslab[c]
    mu = jnp.sum(s1, axis=-1, keepdims=True) * (1.0 / (nsl * LANES))
    s2 = jnp.square(cslab[0] - mu)
    for c in range(1, nsl):
        s2 = s2 + jnp.square(cslab[c] - mu)
    var = jnp.sum(s2, axis=-1, keepdims=True) * (1.0 / (nsl * LANES))
    rstd = lax.rsqrt(var + LN_EPS)
    for c in range(nsl):
        cs = slice(c * LANES, (c + 1) * LANES)
        v = (cslab[c] - mu) * rstd * ng_ref[:, cs] + nb_ref[:, cs]
        lhs[:, cs] = _silu(v).astype(BF16)
    h = _dot(lhs[...], w_ref[...]) + bo_ref[...]
    o_ref[...] = _layer_norm(ALPHA * res_ref[...] + h, g_ref[...], b_ref[...])


def _conv_out_ln(glu, res, w_dw, b_dw, norm_g, norm_b, w_out, b_out, ln_g, ln_b, seq):
    t, d = glu.shape
    width = w_dw.shape[0]
    assert (width - 1) // 2 <= HALO_ROWS and d % LANES == 0
    ts = _tile(seq, 256)
    nsl = d // LANES
    hb = ts // HALO_ROWS
    wdw = w_dw.reshape(width, nsl, LANES).transpose(1, 0, 2)
    bdw = b_dw.reshape(nsl, 1, LANES)
    row = lambda v: v.reshape(1, d)
    vec = pl.BlockSpec((1, d), lambda i: (0, 0))
    return pl.pallas_call(
        functools.partial(_conv_out_body, seq=seq, ts=ts, width=width),
        grid=(t // ts,),
        in_specs=[pl.BlockSpec((ts, d), lambda i: (i, 0)),
                  pl.BlockSpec((HALO_ROWS, d), lambda i: (jnp.maximum(i * hb - 1, 0), 0)),
                  pl.BlockSpec((HALO_ROWS, d), lambda i: (jnp.minimum((i + 1) * hb, t // HALO_ROWS - 1), 0)),
                  pl.BlockSpec((ts, d), lambda i: (i, 0)),
                  pl.BlockSpec((nsl, width, LANES), lambda i: (0, 0, 0)),
                  pl.BlockSpec((nsl, 1, LANES), lambda i: (0, 0, 0)),
                  vec, vec,
                  pl.BlockSpec((d, d), lambda i: (0, 0)),
                  vec, vec, vec],
        out_specs=pl.BlockSpec((ts, d), lambda i: (i, 0)),
        out_shape=jax.ShapeDtypeStruct((t, d), F32),
        scratch_shapes=[pltpu.VMEM((nsl, ts + 2 * HALO_ROWS, LANES), F32),
                        pltpu.VMEM((nsl, ts, LANES), F32),
                        pltpu.VMEM((ts, d), BF16)],
        compiler_params=_cparams(48, 1),
        name="conv_out_ln",
    )(glu, glu, glu, res, wdw, bdw, row(norm_g), row(norm_b), w_out, row(b_out), row(ln_g), row(ln_b))


def _proj_ln_body(*refs, mode, has_bias):
    if mode == "lru":
        hf_ref, hb_ref, y_ref = refs[:3]
        rest = refs[3:]
        lhs = ((hf_ref[...] + hb_ref[...]) * y_ref[...].astype(F32)).astype(BF16)
    else:
        rest = refs[1:]
        lhs = refs[0][...].astype(BF16)
    if has_bias:
        res_ref, w_ref, bo_ref, g_ref, b_ref, o_ref = rest
    else:
        res_ref, w_ref, g_ref, b_ref, o_ref = rest
    h = _dot(lhs, w_ref[...])
    if has_bias:
        h = h + bo_ref[...]
    o_ref[...] = _layer_norm(ALPHA * res_ref[...] + h, g_ref[...], b_ref[...])


def _proj_ln(lhs_args, res, w, b_out, ln_g, ln_b, *, mode="plain"):
    t, d = res.shape
    k = w.shape[0]
    tm = _tile(t, 256)
    tok = lambda n: pl.BlockSpec((tm, n), lambda i: (i, 0))
    vec = pl.BlockSpec((1, d), lambda i: (0, 0))
    row = lambda v: v.reshape(1, d)
    in_specs = [tok(k) for _ in lhs_args] + [tok(d), pl.BlockSpec((k, d), lambda i: (0, 0))]
    args = list(lhs_args) + [res, w]
    if b_out is not None:
        in_specs.append(vec)
        args.append(row(b_out))
    in_specs += [vec, vec]
    args += [row(ln_g), row(ln_b)]
    return pl.pallas_call(
        functools.partial(_proj_ln_body, mode=mode, has_bias=b_out is not None),
        grid=(t // tm,),
        in_specs=in_specs,
        out_specs=tok(d),
        out_shape=jax.ShapeDtypeStruct((t, d), F32),
        compiler_params=_cparams(48, 1),
        name="proj_ln_" + mode,
    )(*args)


def _ffn_body(x_ref, wg_ref, wu_ref, wd_ref, g_ref, b_ref, o_ref, xb, acc):
    f = pl.program_id(1)

    @pl.when(f == 0)
    def _():
        xb[...] = x_ref[...].astype(BF16)
        acc[...] = jnp.zeros_like(acc)

    g = _dot(xb[...], wg_ref[...])
    u = _dot(xb[...], wu_ref[...])
    acc[...] += _dot((_silu(g) * u).astype(BF16), wd_ref[...])

    @pl.when(f == pl.num_programs(1) - 1)
    def _():
        o_ref[...] = _layer_norm(ALPHA * x_ref[...] + acc[...], g_ref[...], b_ref[...])


def _ffn_ln(x, w_gate, w_up, w_down, ln_g, ln_b):
    t, d = x.shape
    fdim = w_gate.shape[1]
    tm, tf = _tile(t, 512), _tile(fdim, 512)
    vec = pl.BlockSpec((1, d), lambda i, f: (0, 0))
    return pl.pallas_call(
        _ffn_body,
        grid=(t // tm, fdim // tf),
        in_specs=[pl.BlockSpec((tm, d), lambda i, f: (i, 0)),
                  pl.BlockSpec((d, tf), lambda i, f: (0, f)),
                  pl.BlockSpec((d, tf), lambda i, f: (0, f)),
                  pl.BlockSpec((tf, d), lambda i, f: (f, 0)),
                  vec, vec],
        out_specs=pl.BlockSpec((tm, d), lambda i, f: (i, 0)),
        out_shape=jax.ShapeDtypeStruct((t, d), F32),
        scratch_shapes=[pltpu.VMEM((tm, d), BF16), pltpu.VMEM((tm, d), F32)],
        compiler_params=_cparams(48, 2),
        name="ffn_ln",
    )(x, w_gate, w_up, w_down, ln_g.reshape(1, d), ln_b.reshape(1, d))


def _diff_attn_body(sc_ref, q_ref, k_ref, v_ref, sg_ref, o_ref, acc1, acc2, m1, l1, m2, l2,
                    *, seq, tq, tk, out_scale):
    hd = DIFF_HEAD_DIM
    h = pl.program_id(1)
    i = pl.program_id(2)
    lam = sc_ref[0]
    slope = sc_ref[1 + h]
    for m_ref, l_ref, a_ref in ((m1, l1, acc1), (m2, l2, acc2)):
        m_ref[...] = jnp.full_like(m_ref, -jnp.inf)
        l_ref[...] = jnp.zeros_like(l_ref)
        a_ref[...] = jnp.zeros_like(a_ref)
    q = q_ref[...]
    rel = (lax.broadcasted_iota(I32, (tq, tk), 0) - lax.broadcasted_iota(I32, (tq, tk), 1)
           + i * tq).astype(F32)

    def step(kt, carry):
        k0 = pl.multiple_of(kt * tk, tk)
        kk = k_ref[pl.ds(k0, tk), :]
        vv = v_ref[pl.ds(k0, tk), :]
        bias = slope * jnp.abs(rel - k0.astype(F32))
        for c, (m_ref, l_ref, a_ref) in enumerate(((m1, l1, acc1), (m2, l2, acc2))):
            s = _dot_nt(q[:, c * hd:(c + 1) * hd], kk[:, c * hd:(c + 1) * hd]) - bias
            m_old = m_ref[...]
            m_new = jnp.maximum(m_old, jnp.max(s, axis=-1, keepdims=True))
            corr = jnp.exp(m_old - m_new)
            p = jnp.exp(s - m_new)
            l_ref[...] = corr * l_ref[...] + jnp.sum(p, axis=-1, keepdims=True)
            a_ref[...] = corr * a_ref[...] + _dot(p.astype(BF16), vv)
            m_ref[...] = m_new
        return carry

    lax.fori_loop(0, seq // tk, step, 0)
    o = acc1[...] / l1[...] - lam * (acc2[...] / l2[...])
    o = o * lax.rsqrt(jnp.mean(o * o, axis=-1, keepdims=True) + SUBLN_EPS) * sg_ref[...]
    o_ref[...] = (o * out_scale).astype(o_ref.dtype)


def _diff_attn(qkv, scalars, subln_g, batch, seq, n_heads, out_scale):
    t = qkv.shape[0]
    vd = 2 * DIFF_HEAD_DIM
    tq, tk = _tile(seq, 256), _tile(seq, 512)
    nq = seq // tq
    return pl.pallas_call(
        functools.partial(_diff_attn_body, seq=seq, tq=tq, tk=tk, out_scale=out_scale),
        grid=(batch, n_heads, nq),
        in_specs=[pl.BlockSpec(memory_space=pltpu.SMEM),
                  pl.BlockSpec((tq, vd), lambda b, h, i: (b * nq + i, h)),
                  pl.BlockSpec((seq, vd), lambda b, h, i: (b, n_heads + h)),
                  pl.BlockSpec((seq, vd), lambda b, h, i: (b, 2 * n_heads + h)),
                  pl.BlockSpec((1, vd), lambda b, h, i: (0, 0))],
        out_specs=pl.BlockSpec((tq, vd), lambda b, h, i: (b * nq + i, h)),
        out_shape=jax.ShapeDtypeStruct((t, n_heads * vd), BF16),
        scratch_shapes=[pltpu.VMEM((tq, vd), F32), pltpu.VMEM((tq, vd), F32)]
        + [pltpu.VMEM((tq, 1), F32) for _ in range(4)],
        compiler_params=_cparams(40, 3),
        name="diff_attn",
    )(scalars, qkv, qkv, qkv, subln_g.reshape(1, vd))


def _win_attn_body(sc_ref, q_ref, kp_ref, kc_ref, kn_ref, vp_ref, vc_ref, vn_ref, o_ref,
                   *, nb, n_heads):
    d, blk = SWA_HEAD_DIM, SWA_BLOCK
    n = pl.program_id(1)
    grp = n_heads // SWA_KV_HEADS
    r = lax.broadcasted_iota(I32, (blk, 3 * blk), 0)
    c = lax.broadcasted_iota(I32, (blk, 3 * blk), 1)
    rel = jnp.abs(r - c + blk)
    valid = (rel <= blk) & ((c >= blk) | (n > 0)) & ((c < 2 * blk) | (n < nb - 1))
    relf = rel.astype(F32)
    for kv in range(SWA_KV_HEADS):
        cs = slice(kv * d, (kv + 1) * d)
        kw = jnp.concatenate([kp_ref[:, cs], kc_ref[:, cs], kn_ref[:, cs]], axis=0)
        vw = jnp.concatenate([vp_ref[:, cs], vc_ref[:, cs], vn_ref[:, cs]], axis=0)
        for g in range(grp):
            h = kv * grp + g
            slope = sc_ref[h]
            sink = sc_ref[n_heads + h]
            s = _dot_nt(q_ref[:, h * d:(h + 1) * d], kw)
            s = jnp.where(valid, s - slope * relf, -jnp.inf)
            m = jnp.maximum(jnp.max(s, axis=-1, keepdims=True), sink)
            p = jnp.exp(s - m)
            p = p / (jnp.sum(p, axis=-1, keepdims=True) + jnp.exp(sink - m))
            o_ref[:, h * d:(h + 1) * d] = _dot(p.astype(BF16), vw).astype(o_ref.dtype)


def _win_attn(qkv, scalars, batch, seq, n_heads):
    t = qkv.shape[0]
    d, blk = SWA_HEAD_DIM, SWA_BLOCK
    nb = seq // blk
    qw, kvw = n_heads * d, SWA_KV_HEADS * d
    assert qw % kvw == 0
    kcol, vcol = qw // kvw, qw // kvw + 1
    prev = lambda b, n: b * nb + jnp.maximum(n - 1, 0)
    nxt = lambda b, n: b * nb + jnp.minimum(n + 1, nb - 1)
    cur = lambda b, n: b * nb + n
    kvspec = lambda f, col: pl.BlockSpec((blk, kvw), lambda b, n: (f(b, n), col))
    return pl.pallas_call(
        functools.partial(_win_attn_body, nb=nb, n_heads=n_heads),
        grid=(batch, nb),
        in_specs=[pl.BlockSpec(memory_space=pltpu.SMEM),
                  pl.BlockSpec((blk, qw), lambda b, n: (cur(b, n), 0)),
                  kvspec(prev, kcol), kvspec(cur, kcol), kvspec(nxt, kcol),
                  kvspec(prev, vcol), kvspec(cur, vcol), kvspec(nxt, vcol)],
        out_specs=pl.BlockSpec((blk, qw), lambda b, n: (cur(b, n), 0)),
        out_shape=jax.ShapeDtypeStruct((t, qw), BF16),
        compiler_params=_cparams(32, 2),
        name="win_attn",
    )(scalars, qkv, qkv, qkv, qkv, qkv, qkv, qkv)


def _lru_body(r_hbm, cw_ref, cb_ref, gw_ref, gb_ref, c_ref, h_hbm,
              rbuf, hout, abuf, bbuf, hcar, isem, osem, *, seq, ts, batch, reverse, left):
    i = pl.program_id(0)
    nch = seq // ts
    taps = cw_ref.shape[0]
    right = taps - 1 - left
    hal = LRU_HALO
    width = r_hbm.shape[2]
    nblk, bw = gw_ref.shape[1], gw_ref.shape[2]
    chunk = lambda s: (nch - 1 - s) if reverse else s
    slot = i % 2

    def in_copies(ci, sl, op):
        t0 = ci * ts

        def each(src_start, dst_start, rows):
            src_start = pl.multiple_of(src_start, LRU_HALO)
            for b in range(batch):
                cp = pltpu.make_async_copy(r_hbm.at[b, pl.ds(src_start, rows), :],
                                           rbuf.at[sl, pl.ds(dst_start, rows), b, :], isem.at[sl])
                getattr(cp, op)()

        each(t0, hal, ts)
        pl.when(ci > 0)(lambda: each(jnp.maximum(t0 - hal, 0), 0, hal))
        pl.when(ci < nch - 1)(lambda: each(jnp.minimum(t0 + ts, seq - hal), hal + ts, hal))

    def out_copies(ci, sl, op):
        for b in range(batch):
            cp = pltpu.make_async_copy(hout.at[sl, pl.ds(0, ts), b, :],
                                       h_hbm.at[b, pl.ds(ci * ts, ts), :], osem.at[sl])
            getattr(cp, op)()

    @pl.when(i == 0)
    def _():
        hcar[...] = jnp.zeros_like(hcar)
        in_copies(chunk(0), 0, "start")

    @pl.when(i + 1 < nch)
    def _():
        in_copies(chunk(i + 1), 1 - slot, "start")

    ci = chunk(i)
    in_copies(ci, slot, "wait")

    @pl.when(ci == 0)
    def _():
        rbuf[slot, hal - left:hal] = jnp.zeros((left, batch, width), F32)

    @pl.when(ci == nch - 1)
    def _():
        rbuf[slot, hal + ts:hal + ts + right] = jnp.zeros((right, batch, width), F32)

    rf = jnp.broadcast_to(cb_ref[...], (ts, batch, width))
    for k in range(taps):
        rf = rf + cw_ref[k:k + 1, :] * rbuf[slot, pl.ds(hal - left + k, ts)]
    rf2 = rf.reshape(ts * batch, width)
    for nbk in range(nblk):
        cs = slice(nbk * bw, (nbk + 1) * bw)
        xb = rf2[:, cs]
        xbb = xb.astype(BF16)
        rg = _sigmoid(_dot(xbb, gw_ref[0, nbk]) + gb_ref[0:1, cs])
        ig = _sigmoid(_dot(xbb, gw_ref[1, nbk]) + gb_ref[1:2, cs])
        a = jnp.exp(c_ref[:, cs] * rg)
        bb = jnp.sqrt(1.0 - a * a) * (ig * xb)
        abuf[:, :, cs] = a.reshape(ts, batch, bw)
        bbuf[:, :, cs] = bb.reshape(ts, batch, bw)

    @pl.when(i >= 2)
    def _():
        out_copies(ci, slot, "wait")

    def step(s, h):
        t = (ts - 1 - s) if reverse else s
        h = abuf[t] * h + bbuf[t]
        hout[slot, t] = h
        return h

    hcar[...] = lax.fori_loop(0, ts, step, hcar[...])
    out_copies(ci, slot, "start")

    @pl.when(i == nch - 1)
    def _():
        out_copies(ci, slot, "wait")
        if nch >= 2:
            out_copies(ci, 1 - slot, "wait")


def _lru_dir(r3, conv_w, conv_b, gate_w, gate_b, cvec, *, reverse, left):
    batch, seq, width = r3.shape
    ts = _tile(seq, 512 // batch)
    taps = conv_w.shape[0]
    assert batch % 8 == 0 and ts % LRU_HALO == 0 and max(left, taps - 1 - left) <= LRU_HALO
    full = lambda a: pl.BlockSpec(a.shape, lambda i: (0,) * a.ndim)
    cb = conv_b.reshape(1, width)
    return pl.pallas_call(
        functools.partial(_lru_body, seq=seq, ts=ts, batch=batch, reverse=reverse, left=left),
        grid=(seq // ts,),
        in_specs=[pl.BlockSpec(memory_space=pl.ANY), full(conv_w), full(cb), full(gate_w),
                  full(gate_b), full(cvec)],
        out_specs=pl.BlockSpec(memory_space=pl.ANY),
        out_shape=jax.ShapeDtypeStruct((batch, seq, width), F32),
        scratch_shapes=[pltpu.VMEM((2, ts + 2 * LRU_HALO, batch, width), F32),
                        pltpu.VMEM((2, ts, batch, width), F32),
                        pltpu.VMEM((ts, batch, width), F32),
                        pltpu.VMEM((ts, batch, width), F32),
                        pltpu.VMEM((batch, width), F32),
                        pltpu.SemaphoreType.DMA((2,)),
                        pltpu.SemaphoreType.DMA((2,))],
        compiler_params=_cparams(48, 1),
        name="lru_rev" if reverse else "lru_fwd",
    )(r3, conv_w, cb, gate_w, gate_b, cvec)


def _split_bf16(x):
    hi = x.astype(BF16)
    lo = (x - hi.astype(F32)).astype(BF16)
    return hi, lo


def _router_body(x_ref, wr_ref, o_ref, cnt_ref, cnt):
    i = pl.program_id(0)
    tm = x_ref.shape[0]
    ne = wr_ref.shape[1]

    @pl.when(i == 0)
    def _():
        cnt[...] = jnp.zeros_like(cnt)

    xh, xl = _split_bf16(x_ref[...])
    wh, wl = _split_bf16(wr_ref[...])
    logits = _dot(xh, wh) + _dot(xl, wh) + _dot(xh, wl)
    lane = lax.broadcasted_iota(I32, (tm, ne), 1).astype(F32)
    v1 = jnp.max(logits, axis=-1, keepdims=True)
    i1 = jnp.min(jnp.where(logits == v1, lane, float(ne)), axis=-1, keepdims=True)
    rest = jnp.where(lane == i1, -jnp.inf, logits)
    v2 = jnp.max(rest, axis=-1, keepdims=True)
    i2 = jnp.min(jnp.where(rest == v2, lane, float(ne)), axis=-1, keepdims=True)
    e2 = jnp.exp(v2 - v1)
    w1 = 1.0 / (1.0 + e2)
    w2 = e2 / (1.0 + e2)
    oh1 = lane == i1
    oh2 = lane == i2
    oh = jnp.where(oh1 | oh2, 1.0, 0.0)
    strict_lower = jnp.where(lax.broadcasted_iota(I32, (tm, tm), 1) < lax.broadcasted_iota(I32, (tm, tm), 0),
                             1.0, 0.0).astype(BF16)
    before = _dot(strict_lower, oh.astype(BF16)) + cnt[...]
    r1 = jnp.sum(jnp.where(oh1, before, 0.0), axis=-1, keepdims=True)
    r2 = jnp.sum(jnp.where(oh2, before, 0.0), axis=-1, keepdims=True)
    cnt[...] += jnp.sum(oh, axis=0, keepdims=True)
    cnt_ref[...] = cnt[...]
    col = lax.broadcasted_iota(I32, (tm, 8), 1)
    out = jnp.zeros((tm, 8), F32)
    for j, v in enumerate((i1.astype(F32), i2.astype(F32), w1, w2, r1, r2)):
        out = jnp.where(col == j, v, out)
    o_ref[...] = out


def _router(x, w_router):
    t, d = x.shape
    ne = w_router.shape[1]
    tm = _tile(t, 512)
    return pl.pallas_call(
        _router_body,
        grid=(t // tm,),
        in_specs=[pl.BlockSpec((tm, d), lambda i: (i, 0)),
                  pl.BlockSpec((d, ne), lambda i: (0, 0))],
        out_specs=[pl.BlockSpec((tm, 8), lambda i: (i, 0)),
                   pl.BlockSpec((1, ne), lambda i: (0, 0))],
        out_shape=[jax.ShapeDtypeStruct((t, 8), F32), jax.ShapeDtypeStruct((1, ne), F32)],
        scratch_shapes=[pltpu.VMEM((1, ne), F32)],
        compiler_params=_cparams(32, 1),
        name="router",
    )(x, w_router)


def _start_row_gather(idx_ref, n_rows, src_hbm, dst, sem):
    def issue(r, carry):
        row = idx_ref[0, 0, r]
        pltpu.make_async_copy(src_hbm.at[pl.ds(row, 1), :], dst.at[pl.ds(r, 1), :], sem).start()
        return carry
    lax.fori_loop(0, n_rows, issue, 0)


def _wait_row_gather(n_rows, src_hbm, dst, sem):
    pltpu.make_async_copy(src_hbm.at[pl.ds(0, n_rows), :], dst, sem).wait()


def _expert_body(te_ref, nv_ref, tok_cur, tok_nxt, gate_ref, x_hbm, wg_ref, wu_ref, wd_ref,
                 o_ref, xbuf, sem):
    j = pl.program_id(0)
    nv = nv_ref[0]
    tm = xbuf.shape[1]
    slot = j % 2

    @pl.when(j == 0)
    def _():
        _start_row_gather(tok_cur, tm, x_hbm, xbuf.at[0], sem.at[0])

    @pl.when(j + 1 < nv)
    def _():
        _start_row_gather(tok_nxt, tm, x_hbm, xbuf.at[1 - slot], sem.at[1 - slot])

    @pl.when(j < nv)
    def _():
        _wait_row_gather(tm, x_hbm, xbuf.at[slot], sem.at[slot])
        xb = xbuf[slot].astype(BF16)
        g = _dot(xb, wg_ref[0])
        u = _dot(xb, wu_ref[0])
        h = (_silu(g) * u * gate_ref[...]).astype(BF16)
        o_ref[...] = _dot(h, wd_ref[0])

    @pl.when(j >= nv)
    def _():
        o_ref[...] = jnp.zeros_like(o_ref)


def _experts(x, tile_expert, n_valid, src_tok, gate_sorted, w_gate, w_up, w_down, tm):
    t, d = x.shape
    ne, _, fdim = w_gate.shape
    nt = src_tok.shape[0]
    tok_spec = lambda f: pl.BlockSpec((1, 1, tm), lambda j, te, nv: (f(j), 0, 0), memory_space=pltpu.SMEM)
    grid_spec = pltpu.PrefetchScalarGridSpec(
        num_scalar_prefetch=2,
        grid=(nt,),
        in_specs=[tok_spec(lambda j: j), tok_spec(lambda j: jnp.minimum(j + 1, nt - 1)),
                  pl.BlockSpec((tm, 1), lambda j, te, nv: (j, 0)),
                  pl.BlockSpec(memory_space=pl.ANY),
                  pl.BlockSpec((1, d, fdim), lambda j, te, nv: (te[j], 0, 0)),
                  pl.BlockSpec((1, d, fdim), lambda j, te, nv: (te[j], 0, 0)),
                  pl.BlockSpec((1, fdim, d), lambda j, te, nv: (te[j], 0, 0))],
        out_specs=pl.BlockSpec((tm, d), lambda j, te, nv: (j, 0)),
        scratch_shapes=[pltpu.VMEM((2, tm, d), F32), pltpu.SemaphoreType.DMA((2,))],
    )
    return pl.pallas_call(
        _expert_body,
        grid_spec=grid_spec,
        out_shape=jax.ShapeDtypeStruct((nt * tm, d), F32),
        compiler_params=_cparams(56, 1),
        name="experts",
    )(tile_expert, n_valid, src_tok, src_tok, gate_sorted, x, w_gate, w_up, w_down)


def _combine_body(pos_cur, pos_nxt, x_ref, g_ref, b_ref, ys_hbm, o_ref, buf, sem):
    i = pl.program_id(0)
    n = pl.num_programs(0)
    tm = x_ref.shape[0]
    slot = i % 2

    @pl.when(i == 0)
    def _():
        _start_row_gather(pos_cur, TOP_K * tm, ys_hbm, buf.at[0], sem.at[0])

    @pl.when(i + 1 < n)
    def _():
        _start_row_gather(pos_nxt, TOP_K * tm, ys_hbm, buf.at[1 - slot], sem.at[1 - slot])

    _wait_row_gather(TOP_K * tm, ys_hbm, buf.at[slot], sem.at[slot])
    f = buf[slot, 0:tm, :] + buf[slot, tm:2 * tm, :]
    o_ref[...] = _layer_norm(ALPHA * x_ref[...] + f, g_ref[...], b_ref[...])


def _combine_ln(x, ys, pos, ln_g, ln_b, tm):
    t, d = x.shape
    n = t // tm
    pos_spec = lambda f: pl.BlockSpec((1, 1, TOP_K * tm), lambda i: (f(i), 0, 0), memory_space=pltpu.SMEM)
    vec = pl.BlockSpec((1, d), lambda i: (0, 0))
    return pl.pallas_call(
        _combine_body,
        grid=(n,),
        in_specs=[pos_spec(lambda i: i), pos_spec(lambda i: jnp.minimum(i + 1, n - 1)),
                  pl.BlockSpec((tm, d), lambda i: (i, 0)), vec, vec,
                  pl.BlockSpec(memory_space=pl.ANY)],
        out_specs=pl.BlockSpec((tm, d), lambda i: (i, 0)),
        out_shape=jax.ShapeDtypeStruct((t, d), F32),
        scratch_shapes=[pltpu.VMEM((2, TOP_K * tm, d), F32), pltpu.SemaphoreType.DMA((2,))],
        compiler_params=_cparams(40, 1),
        name="moe_combine_ln",
    )(pos, pos, x, ln_g.reshape(1, d), ln_b.reshape(1, d), ys)


def _moe_ln(x, w_router, w_gate, w_up, w_down, ln_g, ln_b):
    t, d = x.shape
    ne = w_router.shape[1]
    tm_e = _tile(t, 512)
    tm_c = _tile(t, 256)
    route, counts = _router(x, w_router)
    expert = route[:, 0:2].astype(I32)
    weight = route[:, 2:4]
    rank = route[:, 4:6].astype(I32)
    counts = counts[0].astype(I32)
    padded = ((counts + tm_e - 1) // tm_e) * tm_e
    ends = jnp.cumsum(padded)
    pos = (ends - padded)[expert] + rank
    nt = (TOP_K * t) // tm_e + ne
    tile_expert = jnp.minimum(
        jnp.searchsorted(ends, jnp.arange(nt, dtype=I32) * tm_e, side="right"), ne - 1).astype(I32)
    n_valid = (ends[-1] // tm_e).astype(I32).reshape(1)
    flat = pos.reshape(-1)
    src_tok = jnp.zeros((nt * tm_e,), I32).at[flat].set(jnp.repeat(jnp.arange(t, dtype=I32), TOP_K))
    gate_sorted = jnp.zeros((nt * tm_e,), F32).at[flat].set(weight.reshape(-1))
    ys = _experts(x, tile_expert, n_valid, src_tok.reshape(nt, 1, tm_e), gate_sorted.reshape(-1, 1),
                  w_gate, w_up, w_down, tm_e)
    pos_tiles = pos.reshape(t // tm_c, tm_c, TOP_K).transpose(0, 2, 1).reshape(t // tm_c, 1, TOP_K * tm_c)
    return _combine_ln(x, ys, pos_tiles, ln_g, ln_b, tm_c)


def _alibi_slopes(n_heads):
    return 2.0 ** (-8.0 * jnp.arange(1, n_heads + 1, dtype=F32) / n_heads)


def kernel(x_prompt, x_sample, l0_conv_w_in, l0_conv_b_in, l0_conv_w_dw, l0_conv_b_dw, l0_conv_norm_g, l0_conv_norm_b, l0_conv_w_out, l0_conv_b_out, l0_ln1_g, l0_ln1_b, l0_ffn_w_gate, l0_ffn_w_up, l0_ffn_w_down, l0_ln2_g, l0_ln2_b, l1_attn_w_qkv, l1_attn_lambda, l1_attn_subln_g, l1_attn_w_out, l1_ln1_g, l1_ln1_b, l1_moe_w_router, l1_moe_w_gate, l1_moe_w_up, l1_moe_w_down, l1_ln2_g, l1_ln2_b, l2_rec_w_in, l2_rec_b_in, l2_rec_conv_w, l2_rec_conv_b, l2_rec_gate_w, l2_rec_gate_b, l2_rec_lambda, l2_rec_w_out, l2_rec_b_out, l2_ln1_g, l2_ln1_b, l2_ffn_w_gate, l2_ffn_w_up, l2_ffn_w_down, l2_ln2_g, l2_ln2_b, l3_attn_w_qkv, l3_attn_sink, l3_attn_w_out, l3_ln1_g, l3_ln1_b, l3_moe_w_router, l3_moe_w_gate, l3_moe_w_up, l3_moe_w_down, l3_ln2_g, l3_ln2_b):
    d = x_prompt.shape[-1]
    bf = lambda w: w.astype(BF16)

    diff_w = l1_attn_w_qkv.shape[1] // 3
    n_diff = diff_w // (2 * DIFF_HEAD_DIM)
    col_scale = jnp.concatenate([jnp.full((diff_w,), DIFF_HEAD_DIM ** -0.5, F32), jnp.ones((2 * diff_w,), F32)])
    w_qkv1 = bf(l1_attn_w_qkv * col_scale)
    lam = l1_attn_lambda.astype(F32)
    lambda_init = 0.8 - 0.6 * math.exp(-0.3 * 1)
    lam_full = jnp.exp(jnp.sum(lam[0] * lam[1])) - jnp.exp(jnp.sum(lam[2] * lam[3])) + lambda_init
    diff_scalars = jnp.concatenate([lam_full.reshape(1), _alibi_slopes(n_diff)])

    lru_w = l2_rec_w_out.shape[0]
    lru_left = l2_rec_conv_w.shape[0] // 2
    lru_c = -LRU_C * jax.nn.softplus(-l2_rec_lambda.astype(F32))
    gate_w = bf(l2_rec_gate_w)

    n_swa = d // SWA_HEAD_DIM
    swa_q = n_swa * SWA_HEAD_DIM
    col_scale3 = jnp.concatenate([jnp.full((swa_q,), SWA_HEAD_DIM ** -0.5, F32),
                                  jnp.ones((l3_attn_w_qkv.shape[1] - swa_q,), F32)])
    w_qkv3 = bf(l3_attn_w_qkv * col_scale3)
    swa_scalars = jnp.concatenate([_alibi_slopes(n_swa), l3_attn_sink.astype(F32)])

    w_in0, w_out0 = bf(l0_conv_w_in), bf(l0_conv_w_out)
    ffn0 = (bf(l0_ffn_w_gate), bf(l0_ffn_w_up), bf(l0_ffn_w_down))
    w_out1 = bf(l1_attn_w_out)
    moe1 = (bf(l1_moe_w_gate), bf(l1_moe_w_up), bf(l1_moe_w_down))
    w_in2, w_out2 = bf(l2_rec_w_in), bf(l2_rec_w_out)
    ffn2 = (bf(l2_ffn_w_gate), bf(l2_ffn_w_up), bf(l2_ffn_w_down))
    w_out3 = bf(l3_attn_w_out)
    moe3 = (bf(l3_moe_w_gate), bf(l3_moe_w_up), bf(l3_moe_w_down))
    b_in0 = l0_conv_b_in.reshape(1, -1)
    b_in2 = l2_rec_b_in.reshape(1, -1)

    def trunk(x3):
        batch, seq, _ = x3.shape
        x = x3.reshape(batch * seq, d)
        glu = _mm_glu(x, w_in0, b_in0)
        x = _conv_out_ln(glu, x, l0_conv_w_dw, l0_conv_b_dw, l0_conv_norm_g, l0_conv_norm_b,
                         w_out0, l0_conv_b_out, l0_ln1_g, l0_ln1_b, seq)
        x = _ffn_ln(x, *ffn0, l0_ln2_g, l0_ln2_b)
        qkv = _mm(x, w_qkv1, None, n_out=3 * diff_w, out_dtype=BF16)
        o = _diff_attn(qkv, diff_scalars, l1_attn_subln_g, batch, seq, n_diff, 1.0 - lambda_init)
        x = _proj_ln([o], x, w_out1, None, l1_ln1_g, l1_ln1_b)
        x = _moe_ln(x, l1_moe_w_router, *moe1, l1_ln2_g, l1_ln2_b)
        y = _mm(x, w_in2, b_in2, n_out=lru_w, act="gelu", out_dtype=BF16)
        r = _mm(x, w_in2, b_in2, n_out=lru_w, col_off=lru_w).reshape(batch, seq, lru_w)
        hs = [_lru_dir(r, l2_rec_conv_w, l2_rec_conv_b, gate_w[dr], l2_rec_gate_b[dr], lru_c[dr:dr + 1],
                       reverse=bool(dr), left=lru_left).reshape(batch * seq, lru_w) for dr in range(2)]
        x = _proj_ln([hs[0], hs[1], y], x, w_out2, l2_rec_b_out, l2_ln1_g, l2_ln1_b, mode="lru")
        x = _ffn_ln(x, *ffn2, l2_ln2_g, l2_ln2_b)
        qkv = _mm(x, w_qkv3, None, n_out=w_qkv3.shape[1], out_dtype=BF16)
        o = _win_attn(qkv, swa_scalars, batch, seq, n_swa)
        x = _proj_ln([o], x, w_out3, None, l3_ln1_g, l3_ln1_b)
        x = _moe_ln(x, l3_moe_w_router, *moe3, l3_ln2_g, l3_ln2_b)
        return x.reshape(batch, seq, d)

    return (trunk(x_prompt), trunk(x_sample))
```

```python
import functools
import math

import jax
import jax.numpy as jnp
from jax import lax
from jax.experimental import pallas as pl
from jax.experimental.pallas import tpu as pltpu

F32 = jnp.float32
BF16 = jnp.bfloat16
I32 = jnp.int32

N_LAYERS = 4
LN_EPS = 1e-5
SUBLN_EPS = 1e-5
ALPHA = (2 * N_LAYERS) ** 0.25
LRU_C = 8.0
DIFF_HEAD_DIM = 128
SWA_HEAD_DIM = 128
SWA_KV_HEADS = 4
SWA_BLOCK = 128
TOP_K = 2
LOG2E = math.log2(math.e)
LANES = 128
HALO_ROWS = 16
LRU_HALO = 8
MIB = 1024 * 1024


def _cparams(vmem_mib, n_axes):
    return pltpu.CompilerParams(
        dimension_semantics=("arbitrary",) * n_axes, vmem_limit_bytes=vmem_mib * MIB)


def _tile(n, pref):
    t = min(n, pref)
    while n % t:
        t //= 2
    return t


def _sigmoid(x):
    return 0.5 * jnp.tanh(0.5 * x) + 0.5


def _silu(x):
    return x * _sigmoid(x)


def _gelu_tanh(x):
    return 0.5 * x * (1.0 + jnp.tanh(math.sqrt(2.0 / math.pi) * (x + 0.044715 * (x * x * x))))


def _dot(a, b):
    return jnp.dot(a, b, preferred_element_type=F32)


def _dot_nt(a, b):
    return lax.dot_general(a, b, (((1,), (1,)), ((), ())), preferred_element_type=F32)


def _layer_norm(y, g, b):
    mu = jnp.mean(y, axis=-1, keepdims=True)
    yc = y - mu
    var = jnp.mean(yc * yc, axis=-1, keepdims=True)
    return yc * lax.rsqrt(var + LN_EPS) * g + b


def _mm_body(x_ref, w_ref, o_ref, xb):
    @pl.when(pl.program_id(1) == 0)
    def _():
        xb[...] = x_ref[...].astype(BF16)

    o_ref[...] = _dot(xb[...], w_ref[...]).astype(o_ref.dtype)


def _mm(x, w, out_dtype):
    m, k = x.shape
    n = w.shape[1]
    tm, tn = _tile(m, 512), _tile(n, 1024)
    return pl.pallas_call(
        _mm_body,
        grid=(m // tm, n // tn),
        in_specs=[pl.BlockSpec((tm, k), lambda i, j: (i, 0)),
                  pl.BlockSpec((k, tn), lambda i, j: (0, j))],
        out_specs=pl.BlockSpec((tm, tn), lambda i, j: (i, j)),
        out_shape=jax.ShapeDtypeStruct((m, n), out_dtype),
        scratch_shapes=[pltpu.VMEM((tm, k), BF16)],
        compiler_params=_cparams(48, 2),
        name="mm",
    )(x, w)


def _mm_pair_body(x_ref, wa_ref, wg_ref, ba_ref, bg_ref, *rest, mode):
    xb = rest[-1]

    @pl.when(pl.program_id(1) == 0)
    def _():
        xb[...] = x_ref[...].astype(BF16)

    a = _dot(xb[...], wa_ref[...]) + ba_ref[...]
    g = _dot(xb[...], wg_ref[...]) + bg_ref[...]
    if mode == "glu":
        rest[0][...] = a * _sigmoid(g)
    else:
        rest[0][...] = _gelu_tanh(a).astype(rest[0].dtype)
        rest[1][...] = g


def _mm_pair(x, w, bias, *, mode):
    m, k = x.shape
    n = w.shape[1] // 2
    tm, tn = _tile(m, 512), _tile(n, 512)
    nj = n // tn
    out_spec = pl.BlockSpec((tm, tn), lambda i, j: (i, j))
    if mode == "glu":
        out_specs, out_shape = out_spec, jax.ShapeDtypeStruct((m, n), F32)
    else:
        out_specs = [out_spec, out_spec]
        out_shape = [jax.ShapeDtypeStruct((m, n), BF16), jax.ShapeDtypeStruct((m, n), F32)]
    return pl.pallas_call(
        functools.partial(_mm_pair_body, mode=mode),
        grid=(m // tm, nj),
        in_specs=[pl.BlockSpec((tm, k), lambda i, j: (i, 0)),
                  pl.BlockSpec((k, tn), lambda i, j: (0, j)),
                  pl.BlockSpec((k, tn), lambda i, j: (0, j + nj)),
                  pl.BlockSpec((1, tn), lambda i, j: (0, j)),
                  pl.BlockSpec((1, tn), lambda i, j: (0, j + nj))],
        out_specs=out_specs,
        out_shape=out_shape,
        scratch_shapes=[pltpu.VMEM((tm, k), BF16)],
        compiler_params=_cparams(48, 2),
        name="mm_" + mode,
    )(x, w, w, bias, bias)


def _conv_out_body(cur_ref, prv_ref, nxt_ref, res_ref, wdw_ref, bdw_ref, ng_ref, nb_ref,
                   w_ref, bo_ref, g_ref, b_ref, o_ref, ext, cslab, lhs, *, seq, ts, width):
    i = pl.program_id(0)
    t0 = i * ts
    first = (t0 % seq) == 0
    last = ((t0 + ts) % seq) == 0
    nsl = cur_ref.shape[1] // LANES
    half = (width - 1) // 2
    for c in range(nsl):
        cs = slice(c * LANES, (c + 1) * LANES)
        ext[c, 0:HALO_ROWS, :] = jnp.where(first, 0.0, prv_ref[:, cs])
        ext[c, HALO_ROWS:HALO_ROWS + ts, :] = cur_ref[:, cs]
        ext[c, HALO_ROWS + ts:, :] = jnp.where(last, 0.0, nxt_ref[:, cs])

    def slab(c, carry):
        wk = wdw_ref[c]
        acc = jnp.broadcast_to(bdw_ref[c], (ts, LANES))
        for k in range(width):
            acc = acc + wk[k:k + 1, :] * ext[c, pl.ds(HALO_ROWS - half + k, ts), :]
        cslab[c] = acc
        return carry

    lax.fori_loop(0, nsl, slab, 0)

    s1 = cslab[0]
    for c in range(1, nsl):
        s1 = s1 + cslab[c]
    mu = jnp.sum(s1, axis=-1, keepdims=True) * (1.0 / (nsl * LANES))
    s2 = jnp.square(cslab[0] - mu)
    for c in range(1, nsl):
        s2 = s2 + jnp.square(cslab[c] - mu)
    var = jnp.sum(s2, axis=-1, keepdims=True) * (1.0 / (nsl * LANES))
    rstd = lax.rsqrt(var + LN_EPS)
    for c in range(nsl):
        cs = slice(c * LANES, (c + 1) * LANES)
        v = (cslab[c] - mu) * rstd * ng_ref[:, cs] + nb_ref[:, cs]
        lhs[:, cs] = _silu(v).astype(BF16)
    h = _dot(lhs[...], w_ref[...]) + bo_ref[...]
    o_ref[...] = _layer_norm(ALPHA * res_ref[...] + h, g_ref[...], b_ref[...])


def _conv_out_ln(glu, res, w_dw, b_dw, norm_g, norm_b, w_out, b_out, ln_g, ln_b, seq):
    t, d = glu.shape
    width = w_dw.shape[0]
    assert (width - 1) // 2 <= HALO_ROWS and d % LANES == 0
    ts = _tile(seq, 256)
    nsl = d // LANES
    hb = ts // HALO_ROWS
    wdw = w_dw.reshape(width, nsl, LANES).transpose(1, 0, 2)
    bdw = b_dw.reshape(nsl, 1, LANES)
    row = lambda v: v.reshape(1, d)
    vec = pl.BlockSpec((1, d), lambda i: (0, 0))
    return pl.pallas_call(
        functools.partial(_conv_out_body, seq=seq, ts=ts, width=width),
        grid=(t // ts,),
        in_specs=[pl.BlockSpec((ts, d), lambda i: (i, 0)),
                  pl.BlockSpec((HALO_ROWS, d), lambda i: (jnp.maximum(i * hb - 1, 0), 0)),
                  pl.BlockSpec((HALO_ROWS, d), lambda i: (jnp.minimum((i + 1) * hb, t // HALO_ROWS - 1), 0)),
                  pl.BlockSpec((ts, d), lambda i: (i, 0)),
                  pl.BlockSpec((nsl, width, LANES), lambda i: (0, 0, 0)),
                  pl.BlockSpec((nsl, 1, LANES), lambda i: (0, 0, 0)),
                  vec, vec,
                  pl.BlockSpec((d, d), lambda i: (0, 0)),
                  vec, vec, vec],
        out_specs=pl.BlockSpec((ts, d), lambda i: (i, 0)),
        out_shape=jax.ShapeDtypeStruct((t, d), F32),
        scratch_shapes=[pltpu.VMEM((nsl, ts + 2 * HALO_ROWS, LANES), F32),
                        pltpu.VMEM((nsl, ts, LANES), F32),
                        pltpu.VMEM((ts, d), BF16)],
        compiler_params=_cparams(48, 1),
        name="conv_out_ln",
    )(glu, glu, glu, res, wdw, bdw, row(norm_g), row(norm_b), w_out, row(b_out), row(ln_g), row(ln_b))


def _proj_ln_body(*refs, mode, has_bias):
    if mode == "lru":
        hf_ref, hb_ref, y_ref = refs[:3]
        rest = refs[3:]
        lhs = ((hf_ref[...] + hb_ref[...]) * y_ref[...].astype(F32)).astype(BF16)
    else:
        rest = refs[1:]
        lhs = refs[0][...].astype(BF16)
    if has_bias:
        res_ref, w_ref, bo_ref, g_ref, b_ref, o_ref = rest
    else:
        res_ref, w_ref, g_ref, b_ref, o_ref = rest
    h = _dot(lhs, w_ref[...])
    if has_bias:
        h = h + bo_ref[...]
    o_ref[...] = _layer_norm(ALPHA * res_ref[...] + h, g_ref[...], b_ref[...])


def _proj_ln(lhs_args, res, w, b_out, ln_g, ln_b, *, mode="plain"):
    t, d = res.shape
    k = w.shape[0]
    tm = _tile(t, 256)
    tok = lambda n: pl.BlockSpec((tm, n), lambda i: (i, 0))
    vec = pl.BlockSpec((1, d), lambda i: (0, 0))
    row = lambda v: v.reshape(1, d)
    in_specs = [tok(k) for _ in lhs_args] + [tok(d), pl.BlockSpec((k, d), lambda i: (0, 0))]
    args = list(lhs_args) + [res, w]
    if b_out is not None:
        in_specs.append(vec)
        args.append(row(b_out))
    in_specs += [vec, vec]
    args += [row(ln_g), row(ln_b)]
    return pl.pallas_call(
        functools.partial(_proj_ln_body, mode=mode, has_bias=b_out is not None),
        grid=(t // tm,),
        in_specs=in_specs,
        out_specs=tok(d),
        out_shape=jax.ShapeDtypeStruct((t, d), F32),
        compiler_params=_cparams(48, 1),
        name="proj_ln_" + mode,
    )(*args)


def _ffn_body(x_ref, wg_ref, wu_ref, wd_ref, g_ref, b_ref, o_ref, xb, acc):
    f = pl.program_id(1)

    @pl.when(f == 0)
    def _():
        xb[...] = x_ref[...].astype(BF16)
        acc[...] = jnp.zeros_like(acc)

    g = _dot(xb[...], wg_ref[...])
    u = _dot(xb[...], wu_ref[...])
    acc[...] += _dot((_silu(g) * u).astype(BF16), wd_ref[...])

    @pl.when(f == pl.num_programs(1) - 1)
    def _():
        o_ref[...] = _layer_norm(ALPHA * x_ref[...] + acc[...], g_ref[...], b_ref[...])


def _ffn_ln(x, w_gate, w_up, w_down, ln_g, ln_b):
    t, d = x.shape
    fdim = w_gate.shape[1]
    tm, tf = _tile(t, 512), _tile(fdim, 512)
    vec = pl.BlockSpec((1, d), lambda i, f: (0, 0))
    return pl.pallas_call(
        _ffn_body,
        grid=(t // tm, fdim // tf),
        in_specs=[pl.BlockSpec((tm, d), lambda i, f: (i, 0)),
                  pl.BlockSpec((d, tf), lambda i, f: (0, f)),
                  pl.BlockSpec((d, tf), lambda i, f: (0, f)),
                  pl.BlockSpec((tf, d), lambda i, f: (f, 0)),
                  vec, vec],
        out_specs=pl.BlockSpec((tm, d), lambda i, f: (i, 0)),
        out_shape=jax.ShapeDtypeStruct((t, d), F32),
        scratch_shapes=[pltpu.VMEM((tm, d), BF16), pltpu.VMEM((tm, d), F32)],
        compiler_params=_cparams(48, 2),
        name="ffn_ln",
    )(x, w_gate, w_up, w_down, ln_g.reshape(1, d), ln_b.reshape(1, d))


DIFF_KEY_RADIX = 256
DIFF_SC_PER_HEAD = 7


def _split3_f32(x):
    hi = x.astype(BF16).astype(F32)
    mid = (x - hi).astype(BF16).astype(F32)
    lo = (x - hi - mid).astype(BF16).astype(F32)
    return hi, mid, lo


def _diff_attn_body(sc_ref, q_ref, k_ref, v_ref, kx_ref, sg_ref, o_ref, qa, vt, acc, mrow, lrow,
                    *, seq, tq, tk, hpb, out_scale):
    hd = DIFF_HEAD_DIM
    vd = 2 * hd
    hblk = pl.program_id(1)
    i = pl.program_id(2)
    i0 = i * tq
    lam = sc_ref[0]
    mrow[...] = jnp.full_like(mrow, -jnp.inf)
    lrow[...] = jnp.zeros_like(lrow)
    acc[...] = jnp.zeros_like(acc)

    @pl.when(i == 0)
    def _():
        for hh in range(hpb):
            for kt in range(seq // tk):
                vt[hh, kt] = v_ref[kt * tk:(kt + 1) * tk, hh * vd:(hh + 1) * vd].astype(F32).T.astype(BF16)

    irel = lax.broadcasted_iota(I32, (tq, 1), 0).astype(F32)
    lane = lax.broadcasted_iota(I32, (tq, hd), 1)
    slopes = []
    for hh in range(hpb):
        base = 1 + DIFF_SC_PER_HEAD * (hblk * hpb + hh)
        slope2 = sc_ref[base + 6]
        slopes.append(slope2)
        ext = jnp.zeros((tq, hd), F32)
        for j, v in enumerate((*_split3_f32(-slope2 * irel), *[sc_ref[base + t] for t in range(6)])):
            ext = jnp.where(lane == j, v, ext)
        for c in range(2):
            qc = q_ref[:, hh * vd + c * hd:hh * vd + (c + 1) * hd]
            qa[hh, c, 0] = jnp.concatenate([qc, ext.astype(BF16)], axis=1)
            qa[hh, c, 1] = jnp.concatenate([qc, (-ext).astype(BF16)], axis=1)

    def step(kt, carry, *, mode):
        j0 = pl.multiple_of(kt * tk, tk)
        c0 = (i0 - j0).astype(F32)
        kk = k_ref[pl.ds(j0, tk), :]
        sgn = 1 if mode == "right" else 0
        if mode == "straddle":
            d = (lax.broadcasted_iota(I32, (tk, tq), 1) - lax.broadcasted_iota(I32, (tk, tq), 0)).astype(F32) + c0
            dneg = 2.0 * jnp.minimum(d, 0.0)
        for hh in range(hpb):
            kappa = slopes[hh] * c0 if mode == "right" else -slopes[hh] * c0
            for c in range(2):
                ka = jnp.concatenate([kk[:, hh * vd + c * hd:hh * vd + (c + 1) * hd], kx_ref[...]], axis=1)
                s = _dot_nt(ka, qa[hh, c, sgn])
                if mode == "straddle":
                    s = s + slopes[hh] * dneg
                m_old = mrow[hh, c]
                m_new = jnp.maximum(m_old, jnp.max(s, axis=0, keepdims=True) + kappa)
                corr = jnp.exp2(m_old - m_new)
                p = jnp.exp2(s - (m_new - kappa))
                lrow[hh, c] = corr * lrow[hh, c] + jnp.sum(p, axis=0, keepdims=True)
                acc[hh, c] = corr * acc[hh, c] + _dot(vt[hh, kt], p.astype(BF16))
                mrow[hh, c] = m_new
        return carry

    n_left = i0 // tk
    n_str = tq // tk
    lax.fori_loop(0, n_left, functools.partial(step, mode="left"), 0)
    lax.fori_loop(n_left, n_left + n_str, functools.partial(step, mode="straddle"), 0)
    lax.fori_loop(n_left + n_str, seq // tk, functools.partial(step, mode="right"), 0)
    for hh in range(hpb):
        o = acc[hh, 0] / lrow[hh, 0] - lam * (acc[hh, 1] / lrow[hh, 1])
        o = o * lax.rsqrt(jnp.mean(o * o, axis=0, keepdims=True) + SUBLN_EPS) * (sg_ref[...] * out_scale)
        o_ref[:, hh * vd:(hh + 1) * vd] = o.T.astype(o_ref.dtype)


def _diff_attn(qkv, scalars, subln_g, batch, seq, n_heads, out_scale):
    t = qkv.shape[0]
    hd = DIFF_HEAD_DIM
    vd = 2 * hd
    hpb = 2 if n_heads % 2 == 0 else 1
    nhb = n_heads // hpb
    tq = tk = _tile(seq, 512)
    assert tk <= DIFF_KEY_RADIX * DIFF_KEY_RADIX
    nq = seq // tq
    lane = jnp.arange(hd)[None, :]
    jrel = jnp.arange(tk)[:, None]
    jl = (jrel % DIFF_KEY_RADIX).astype(F32)
    jh = (jrel // DIFF_KEY_RADIX).astype(F32)
    kx = jnp.where(lane < 3, 1.0, jnp.where(lane < 6, jl, jnp.where(lane < 9, jh, 0.0))).astype(BF16)
    return pl.pallas_call(
        functools.partial(_diff_attn_body, seq=seq, tq=tq, tk=tk, hpb=hpb, out_scale=out_scale),
        grid=(batch, nhb, nq),
        in_specs=[pl.BlockSpec(memory_space=pltpu.SMEM),
                  pl.BlockSpec((tq, hpb * vd), lambda b, h, i: (b * nq + i, h)),
                  pl.BlockSpec((seq, hpb * vd), lambda b, h, i: (b, nhb + h)),
                  pl.BlockSpec((seq, hpb * vd), lambda b, h, i: (b, 2 * nhb + h)),
                  pl.BlockSpec((tk, hd), lambda b, h, i: (0, 0)),
                  pl.BlockSpec((vd, 1), lambda b, h, i: (0, 0))],
        out_specs=pl.BlockSpec((tq, hpb * vd), lambda b, h, i: (b * nq + i, h)),
        out_shape=jax.ShapeDtypeStruct((t, n_heads * vd), BF16),
        scratch_shapes=[pltpu.VMEM((hpb, 2, 2, tq, vd), BF16), pltpu.VMEM((hpb, seq // tk, vd, tk), BF16),
                        pltpu.VMEM((hpb, 2, vd, tq), F32), pltpu.VMEM((hpb, 2, 1, tq), F32),
                        pltpu.VMEM((hpb, 2, 1, tq), F32)],
        compiler_params=_cparams(48, 3),
        name="diff_attn",
    )(scalars, qkv, qkv, qkv, kx, subln_g.reshape(vd, 1))


def _win_attn_body(sc_ref, q_ref, kp_ref, kc_ref, kn_ref, vp_ref, vc_ref, vn_ref, o_ref,
                   *, nb, n_heads):
    d, blk = SWA_HEAD_DIM, SWA_BLOCK
    n = pl.program_id(1)
    grp = n_heads // SWA_KV_HEADS
    rows = grp * blk
    r = lax.broadcasted_iota(I32, (rows, 3 * blk), 0) % blk
    c = lax.broadcasted_iota(I32, (rows, 3 * blk), 1)
    rel = jnp.abs(r - c + blk)
    valid = (rel <= blk) & ((c >= blk) | (n > 0)) & ((c < 2 * blk) | (n < nb - 1))
    relf = rel.astype(F32)
    mask = jnp.where(valid, 0.0, -jnp.inf)
    head_of_row = lax.broadcasted_iota(I32, (rows, 1), 0) // blk
    for kv in range(SWA_KV_HEADS):
        cs = slice(kv * d, (kv + 1) * d)
        kw = jnp.concatenate([kp_ref[:, cs], kc_ref[:, cs], kn_ref[:, cs]], axis=0)
        vw = jnp.concatenate([vp_ref[:, cs], vc_ref[:, cs], vn_ref[:, cs]], axis=0)
        qg = jnp.concatenate([q_ref[:, (kv * grp + g) * d:(kv * grp + g + 1) * d] for g in range(grp)], axis=0)
        slope = jnp.zeros((rows, 1), F32)
        sink = jnp.zeros((rows, 1), F32)
        for g in range(grp):
            slope = jnp.where(head_of_row == g, sc_ref[kv * grp + g], slope)
            sink = jnp.where(head_of_row == g, sc_ref[n_heads + kv * grp + g], sink)
        s = _dot_nt(qg, kw) - slope * relf + mask
        m = jnp.maximum(jnp.max(s, axis=-1, keepdims=True), sink)
        p = jnp.exp2(s - m)
        denom = jnp.sum(p, axis=-1, keepdims=True) + jnp.exp2(sink - m)
        o = _dot(p.astype(BF16), vw) / denom
        for g in range(grp):
            hcol = (kv * grp + g) * d
            o_ref[:, hcol:hcol + d] = o[g * blk:(g + 1) * blk, :].astype(o_ref.dtype)


def _win_attn(qkv, scalars, batch, seq, n_heads):
    t = qkv.shape[0]
    d, blk = SWA_HEAD_DIM, SWA_BLOCK
    nb = seq // blk
    qw, kvw = n_heads * d, SWA_KV_HEADS * d
    assert qw % kvw == 0
    kcol, vcol = qw // kvw, qw // kvw + 1
    prev = lambda b, n: b * nb + jnp.maximum(n - 1, 0)
    nxt = lambda b, n: b * nb + jnp.minimum(n + 1, nb - 1)
    cur = lambda b, n: b * nb + n
    kvspec = lambda f, col: pl.BlockSpec((blk, kvw), lambda b, n: (f(b, n), col))
    return pl.pallas_call(
        functools.partial(_win_attn_body, nb=nb, n_heads=n_heads),
        grid=(batch, nb),
        in_specs=[pl.BlockSpec(memory_space=pltpu.SMEM),
                  pl.BlockSpec((blk, qw), lambda b, n: (cur(b, n), 0)),
                  kvspec(prev, kcol), kvspec(cur, kcol), kvspec(nxt, kcol),
                  kvspec(prev, vcol), kvspec(cur, vcol), kvspec(nxt, vcol)],
        out_specs=pl.BlockSpec((blk, qw), lambda b, n: (cur(b, n), 0)),
        out_shape=jax.ShapeDtypeStruct((t, qw), BF16),
        compiler_params=_cparams(32, 2),
        name="win_attn",
    )(scalars, qkv, qkv, qkv, qkv, qkv, qkv, qkv)


def _lru_body(r_hbm, cw_ref, cb_ref, gw_ref, gb_ref, c_ref, h_hbm,
              rbuf, hout, abuf, bbuf, hcar, isem, osem, *, seq, ts, batch, reverse, left):
    i = pl.program_id(0)
    nch = seq // ts
    taps = cw_ref.shape[0]
    right = taps - 1 - left
    hal = LRU_HALO
    width = r_hbm.shape[2]
    nblk, bw = gw_ref.shape[1], gw_ref.shape[2]
    chunk = lambda s: (nch - 1 - s) if reverse else s
    slot = i % 2

    def in_copies(ci, sl, op):
        t0 = ci * ts

        def each(src_start, dst_start, rows):
            src_start = pl.multiple_of(src_start, LRU_HALO)
            for b in range(batch):
                cp = pltpu.make_async_copy(r_hbm.at[b, pl.ds(src_start, rows), :],
                                           rbuf.at[sl, pl.ds(dst_start, rows), b, :], isem.at[sl])
                getattr(cp, op)()

        each(t0, hal, ts)
        pl.when(ci > 0)(lambda: each(jnp.maximum(t0 - hal, 0), 0, hal))
        pl.when(ci < nch - 1)(lambda: each(jnp.minimum(t0 + ts, seq - hal), hal + ts, hal))

    def out_copies(ci, sl, op):
        for b in range(batch):
            cp = pltpu.make_async_copy(hout.at[sl, pl.ds(0, ts), b, :],
                                       h_hbm.at[b, pl.ds(ci * ts, ts), :], osem.at[sl])
            getattr(cp, op)()

    @pl.when(i == 0)
    def _():
        hcar[...] = jnp.zeros_like(hcar)
        in_copies(chunk(0), 0, "start")

    @pl.when(i + 1 < nch)
    def _():
        in_copies(chunk(i + 1), 1 - slot, "start")

    ci = chunk(i)
    in_copies(ci, slot, "wait")

    @pl.when(ci == 0)
    def _():
        rbuf[slot, hal - left:hal] = jnp.zeros((left, batch, width), F32)

    @pl.when(ci == nch - 1)
    def _():
        rbuf[slot, hal + ts:hal + ts + right] = jnp.zeros((right, batch, width), F32)

    rf = jnp.broadcast_to(cb_ref[...], (ts, batch, width))
    for k in range(taps):
        rf = rf + cw_ref[k:k + 1, :] * rbuf[slot, pl.ds(hal - left + k, ts)]
    rf2 = rf.reshape(ts * batch, width)
    for nbk in range(nblk):
        cs = slice(nbk * bw, (nbk + 1) * bw)
        xb = rf2[:, cs]
        xbb = xb.astype(BF16)
        rg = _sigmoid(_dot(xbb, gw_ref[0, nbk]) + gb_ref[0:1, cs])
        ig = _sigmoid(_dot(xbb, gw_ref[1, nbk]) + gb_ref[1:2, cs])
        a = jnp.exp(c_ref[:, cs] * rg)
        bb = jnp.sqrt(1.0 - a * a) * (ig * xb)
        abuf[:, :, cs] = a.reshape(ts, batch, bw)
        bbuf[:, :, cs] = bb.reshape(ts, batch, bw)

    @pl.when(i >= 2)
    def _():
        out_copies(ci, slot, "wait")

    def step(s, h):
        t = (ts - 1 - s) if reverse else s
        h = abuf[t] * h + bbuf[t]
        hout[slot, t] = h
        return h

    hcar[...] = lax.fori_loop(0, ts, step, hcar[...])
    out_copies(ci, slot, "start")

    @pl.when(i == nch - 1)
    def _():
        out_copies(ci, slot, "wait")
        if nch >= 2:
            out_copies(ci, 1 - slot, "wait")


def _lru_dir(r3, conv_w, conv_b, gate_w, gate_b, cvec, *, reverse, left):
    batch, seq, width = r3.shape
    ts = _tile(seq, 512 // batch)
    taps = conv_w.shape[0]
    assert batch % 8 == 0 and ts % LRU_HALO == 0 and max(left, taps - 1 - left) <= LRU_HALO
    full = lambda a: pl.BlockSpec(a.shape, lambda i: (0,) * a.ndim)
    cb = conv_b.reshape(1, width)
    return pl.pallas_call(
        functools.partial(_lru_body, seq=seq, ts=ts, batch=batch, reverse=reverse, left=left),
        grid=(seq // ts,),
        in_specs=[pl.BlockSpec(memory_space=pl.ANY), full(conv_w), full(cb), full(gate_w),
                  full(gate_b), full(cvec)],
        out_specs=pl.BlockSpec(memory_space=pl.ANY),
        out_shape=jax.ShapeDtypeStruct((batch, seq, width), F32),
        scratch_shapes=[pltpu.VMEM((2, ts + 2 * LRU_HALO, batch, width), F32),
                        pltpu.VMEM((2, ts, batch, width), F32),
                        pltpu.VMEM((ts, batch, width), F32),
                        pltpu.VMEM((ts, batch, width), F32),
                        pltpu.VMEM((batch, width), F32),
                        pltpu.SemaphoreType.DMA((2,)),
                        pltpu.SemaphoreType.DMA((2,))],
        compiler_params=_cparams(48, 1),
        name="lru_rev" if reverse else "lru_fwd",
    )(r3, conv_w, cb, gate_w, gate_b, cvec)


def _split_bf16(x):
    hi = x.astype(BF16)
    lo = (x - hi.astype(F32)).astype(BF16)
    return hi, lo


def _router_body(x_ref, wr_ref, o_ref, cnt_ref, cnt):
    i = pl.program_id(0)
    tm = x_ref.shape[0]
    ne = wr_ref.shape[1]

    @pl.when(i == 0)
    def _():
        cnt[...] = jnp.zeros_like(cnt)

    xh, xl = _split_bf16(x_ref[...])
    wh, wl = _split_bf16(wr_ref[...])
    logits = _dot(xh, wh) + _dot(xl, wh) + _dot(xh, wl)
    lane = lax.broadcasted_iota(I32, (tm, ne), 1).astype(F32)
    v1 = jnp.max(logits, axis=-1, keepdims=True)
    i1 = jnp.min(jnp.where(logits == v1, lane, float(ne)), axis=-1, keepdims=True)
    rest = jnp.where(lane == i1, -jnp.inf, logits)
    v2 = jnp.max(rest, axis=-1, keepdims=True)
    i2 = jnp.min(jnp.where(rest == v2, lane, float(ne)), axis=-1, keepdims=True)
    e2 = jnp.exp(v2 - v1)
    w1 = 1.0 / (1.0 + e2)
    w2 = e2 / (1.0 + e2)
    oh1 = lane == i1
    oh2 = lane == i2
    oh = jnp.where(oh1 | oh2, 1.0, 0.0)
    strict_lower = jnp.where(lax.broadcasted_iota(I32, (tm, tm), 1) < lax.broadcasted_iota(I32, (tm, tm), 0),
                             1.0, 0.0).astype(BF16)
    before = _dot(strict_lower, oh.astype(BF16)) + cnt[...]
    r1 = jnp.sum(jnp.where(oh1, before, 0.0), axis=-1, keepdims=True)
    r2 = jnp.sum(jnp.where(oh2, before, 0.0), axis=-1, keepdims=True)
    cnt[...] += jnp.sum(oh, axis=0, keepdims=True)
    cnt_ref[...] = cnt[...]
    col = lax.broadcasted_iota(I32, (tm, 8), 1)
    out = jnp.zeros((tm, 8), F32)
    for j, v in enumerate((i1, i2, w1, w2, r1, r2)):
        out = jnp.where(col == j, v, out)
    o_ref[...] = out


def _router(x, w_router):
    t, d = x.shape
    ne = w_router.shape[1]
    tm = _tile(t, 512)
    return pl.pallas_call(
        _router_body,
        grid=(t // tm,),
        in_specs=[pl.BlockSpec((tm, d), lambda i: (i, 0)),
                  pl.BlockSpec((d, ne), lambda i: (0, 0))],
        out_specs=[pl.BlockSpec((tm, 8), lambda i: (i, 0)),
                   pl.BlockSpec((1, ne), lambda i: (0, 0))],
        out_shape=[jax.ShapeDtypeStruct((t, 8), F32), jax.ShapeDtypeStruct((1, ne), F32)],
        scratch_shapes=[pltpu.VMEM((1, ne), F32)],
        compiler_params=_cparams(32, 1),
        name="router",
    )(x, w_router)


ROW_DMA_UNROLL = 8


def _start_row_gather(idx_ref, n_rows, src_hbm, dst, sem):
    def issue(r, carry):
        row = idx_ref[0, 0, r]
        pltpu.make_async_copy(src_hbm.at[pl.ds(row, 1), :], dst.at[pl.ds(r, 1), :], sem).start()
        return carry
    lax.fori_loop(0, n_rows, issue, 0, unroll=ROW_DMA_UNROLL)


def _wait_rows(n_rows, hbm, vmem, sem):
    pltpu.make_async_copy(hbm.at[pl.ds(0, n_rows), :], vmem, sem).wait()


def _dispatch_body(off_ref, pad_ref, nv_ref, pos_ref, x_ref, xs_hbm, zbuf, sem):
    i = pl.program_id(0)
    tm = x_ref.shape[0]
    tz = zbuf.shape[0]

    @pl.when(i == 0)
    def _():
        zbuf[...] = jnp.zeros_like(zbuf)

        def zero_tile(row0):
            row0 = pl.multiple_of(row0, tz)
            cp = pltpu.make_async_copy(zbuf, xs_hbm.at[pl.ds(row0, tz), :], sem.at[1])
            cp.start()
            cp.wait()

        for e in range(off_ref.shape[0]):
            pl.when(pad_ref[e] > 0)(functools.partial(zero_tile, off_ref[e] + pad_ref[e] - tz))

        def tail(j, carry):
            zero_tile(j * tz)
            return carry
        lax.fori_loop(nv_ref[0], xs_hbm.shape[0] // tz, tail, 0)

    def issue(r, carry):
        for slot in range(TOP_K):
            dst = pos_ref[0, 0, slot * tm + r]
            pltpu.make_async_copy(x_ref.at[pl.ds(r, 1), :], xs_hbm.at[pl.ds(dst, 1), :], sem.at[0]).start()
        return carry

    lax.fori_loop(0, tm, issue, 0, unroll=ROW_DMA_UNROLL)
    for _ in range(TOP_K):
        _wait_rows(tm, xs_hbm, x_ref, sem.at[0])


def _dispatch(x, pos_tiles, offsets, padded, n_valid, n_rows, tm, tm_e):
    t, d = x.shape
    grid_spec = pltpu.PrefetchScalarGridSpec(
        num_scalar_prefetch=3,
        grid=(t // tm,),
        in_specs=[pl.BlockSpec((1, 1, TOP_K * tm), lambda i, o, p, nv: (i, 0, 0), memory_space=pltpu.SMEM),
                  pl.BlockSpec((tm, d), lambda i, o, p, nv: (i, 0))],
        out_specs=pl.BlockSpec(memory_space=pl.ANY),
        scratch_shapes=[pltpu.VMEM((tm_e, d), F32), pltpu.SemaphoreType.DMA((2,))],
    )
    return pl.pallas_call(
        _dispatch_body,
        grid_spec=grid_spec,
        out_shape=jax.ShapeDtypeStruct((n_rows, d), F32),
        compiler_params=_cparams(32, 1),
        name="moe_dispatch",
    )(offsets, padded, n_valid, pos_tiles, x)


def _expert_body(te_ref, nv_ref, x_ref, wg_ref, wu_ref, wd_ref, o_ref):
    j = pl.program_id(0)

    @pl.when(j < nv_ref[0])
    def _():
        xb = x_ref[...].astype(BF16)
        g = _dot(xb, wg_ref[0])
        u = _dot(xb, wu_ref[0])
        o_ref[...] = _dot((_silu(g) * u).astype(BF16), wd_ref[0])

    @pl.when(j >= nv_ref[0])
    def _():
        o_ref[...] = jnp.zeros_like(o_ref)


def _experts(xs, tile_expert, n_valid, w_gate, w_up, w_down, tm):
    n_rows, d = xs.shape
    fdim = w_gate.shape[2]
    nt = n_rows // tm
    wspec = lambda shape: pl.BlockSpec((1,) + shape, lambda j, te, nv: (te[j], 0, 0))
    grid_spec = pltpu.PrefetchScalarGridSpec(
        num_scalar_prefetch=2,
        grid=(nt,),
        in_specs=[pl.BlockSpec((tm, d), lambda j, te, nv: (jnp.minimum(j, nv[0] - 1), 0)),
                  wspec((d, fdim)), wspec((d, fdim)), wspec((fdim, d))],
        out_specs=pl.BlockSpec((tm, d), lambda j, te, nv: (j, 0)),
    )
    return pl.pallas_call(
        _expert_body,
        grid_spec=grid_spec,
        out_shape=jax.ShapeDtypeStruct((n_rows, d), F32),
        compiler_params=_cparams(56, 1),
        name="experts",
    )(tile_expert, n_valid, xs, w_gate, w_up, w_down)


def _combine_body(pos_cur, pos_nxt, x_ref, route_ref, g_ref, b_ref, ys_hbm, o_ref, buf, sem):
    i = pl.program_id(0)
    n = pl.num_programs(0)
    tm = x_ref.shape[0]
    slot = i % 2

    @pl.when(i == 0)
    def _():
        _start_row_gather(pos_cur, TOP_K * tm, ys_hbm, buf.at[0], sem.at[0])

    @pl.when(i + 1 < n)
    def _():
        _start_row_gather(pos_nxt, TOP_K * tm, ys_hbm, buf.at[1 - slot], sem.at[1 - slot])

    _wait_rows(TOP_K * tm, ys_hbm, buf.at[slot], sem.at[slot])
    f = route_ref[:, 2:3] * buf[slot, 0:tm, :] + route_ref[:, 3:4] * buf[slot, tm:2 * tm, :]
    o_ref[...] = _layer_norm(ALPHA * x_ref[...] + f, g_ref[...], b_ref[...])


def _combine_ln(x, ys, route, pos, ln_g, ln_b, tm):
    t, d = x.shape
    n = t // tm
    pos_spec = lambda f: pl.BlockSpec((1, 1, TOP_K * tm), lambda i: (f(i), 0, 0), memory_space=pltpu.SMEM)
    vec = pl.BlockSpec((1, d), lambda i: (0, 0))
    return pl.pallas_call(
        _combine_body,
        grid=(n,),
        in_specs=[pos_spec(lambda i: i), pos_spec(lambda i: jnp.minimum(i + 1, n - 1)),
                  pl.BlockSpec((tm, d), lambda i: (i, 0)),
                  pl.BlockSpec((tm, route.shape[1]), lambda i: (i, 0)), vec, vec,
                  pl.BlockSpec(memory_space=pl.ANY)],
        out_specs=pl.BlockSpec((tm, d), lambda i: (i, 0)),
        out_shape=jax.ShapeDtypeStruct((t, d), F32),
        scratch_shapes=[pltpu.VMEM((2, TOP_K * tm, d), F32), pltpu.SemaphoreType.DMA((2,))],
        compiler_params=_cparams(40, 1),
        name="moe_combine_ln",
    )(pos, pos, x, route, ln_g.reshape(1, d), ln_b.reshape(1, d), ys)


def _moe_ln(x, w_router, w_gate, w_up, w_down, ln_g, ln_b):
    t, d = x.shape
    ne = w_router.shape[1]
    tm_e = _tile(t, 512)
    tm_c = _tile(t, 256)
    route, counts = _router(x, w_router)
    expert = route[:, 0:2].astype(I32)
    rank = route[:, 4:6].astype(I32)
    counts = counts[0].astype(I32)
    padded = ((counts + tm_e - 1) // tm_e) * tm_e
    ends = jnp.cumsum(padded)
    offsets = ends - padded
    pos = offsets[expert] + rank
    nt = (TOP_K * t) // tm_e + ne
    tile_expert = jnp.minimum(
        jnp.searchsorted(ends, jnp.arange(nt, dtype=I32) * tm_e, side="right"), ne - 1).astype(I32)
    n_valid = (ends[-1] // tm_e).astype(I32).reshape(1)
    tiles = lambda tm: pos.reshape(t // tm, tm, TOP_K).transpose(0, 2, 1).reshape(t // tm, 1, TOP_K * tm)
    xs = _dispatch(x, tiles(tm_c), offsets, padded, n_valid, nt * tm_e, tm_c, tm_e)
    ys = _experts(xs, tile_expert, n_valid, w_gate, w_up, w_down, tm_e)
    return _combine_ln(x, ys, route, tiles(tm_c), ln_g, ln_b, tm_c)


def _alibi_slopes(n_heads):
    return 2.0 ** (-8.0 * jnp.arange(1, n_heads + 1, dtype=F32) / n_heads)


def kernel(x_prompt, x_sample, l0_conv_w_in, l0_conv_b_in, l0_conv_w_dw, l0_conv_b_dw, l0_conv_norm_g, l0_conv_norm_b, l0_conv_w_out, l0_conv_b_out, l0_ln1_g, l0_ln1_b, l0_ffn_w_gate, l0_ffn_w_up, l0_ffn_w_down, l0_ln2_g, l0_ln2_b, l1_attn_w_qkv, l1_attn_lambda, l1_attn_subln_g, l1_attn_w_out, l1_ln1_g, l1_ln1_b, l1_moe_w_router, l1_moe_w_gate, l1_moe_w_up, l1_moe_w_down, l1_ln2_g, l1_ln2_b, l2_rec_w_in, l2_rec_b_in, l2_rec_conv_w, l2_rec_conv_b, l2_rec_gate_w, l2_rec_gate_b, l2_rec_lambda, l2_rec_w_out, l2_rec_b_out, l2_ln1_g, l2_ln1_b, l2_ffn_w_gate, l2_ffn_w_up, l2_ffn_w_down, l2_ln2_g, l2_ln2_b, l3_attn_w_qkv, l3_attn_sink, l3_attn_w_out, l3_ln1_g, l3_ln1_b, l3_moe_w_router, l3_moe_w_gate, l3_moe_w_up, l3_moe_w_down, l3_ln2_g, l3_ln2_b):
    d = x_prompt.shape[-1]
    bf = lambda w: w.astype(BF16)

    diff_w = l1_attn_w_qkv.shape[1] // 3
    n_diff = diff_w // (2 * DIFF_HEAD_DIM)
    col_scale = jnp.concatenate([jnp.full((diff_w,), LOG2E * DIFF_HEAD_DIM ** -0.5, F32), jnp.ones((2 * diff_w,), F32)])
    w_qkv1 = bf(l1_attn_w_qkv * col_scale)
    lam = l1_attn_lambda.astype(F32)
    lambda_init = 0.8 - 0.6 * math.exp(-0.3 * 1)
    lam_full = jnp.exp(jnp.sum(lam[0] * lam[1])) - jnp.exp(jnp.sum(lam[2] * lam[3])) + lambda_init
    slope2 = LOG2E * _alibi_slopes(n_diff)
    diff_scalars = jnp.concatenate(
        [lam_full.reshape(1),
         jnp.stack([*_split3_f32(slope2), *_split3_f32(DIFF_KEY_RADIX * slope2), slope2], axis=1).reshape(-1)])

    lru_w = l2_rec_w_out.shape[0]
    lru_left = l2_rec_conv_w.shape[0] // 2
    lru_c = -LRU_C * jax.nn.softplus(-l2_rec_lambda.astype(F32))
    gate_w = bf(l2_rec_gate_w)

    n_swa = d // SWA_HEAD_DIM
    swa_q = n_swa * SWA_HEAD_DIM
    col_scale3 = jnp.concatenate([jnp.full((swa_q,), LOG2E * SWA_HEAD_DIM ** -0.5, F32),
                                  jnp.ones((l3_attn_w_qkv.shape[1] - swa_q,), F32)])
    w_qkv3 = bf(l3_attn_w_qkv * col_scale3)
    swa_scalars = LOG2E * jnp.concatenate([_alibi_slopes(n_swa), l3_attn_sink.astype(F32)])

    w_in0, w_out0 = bf(l0_conv_w_in), bf(l0_conv_w_out)
    ffn0 = (bf(l0_ffn_w_gate), bf(l0_ffn_w_up), bf(l0_ffn_w_down))
    w_out1 = bf(l1_attn_w_out)
    moe1 = (bf(l1_moe_w_gate), bf(l1_moe_w_up), bf(l1_moe_w_down))
    w_in2, w_out2 = bf(l2_rec_w_in), bf(l2_rec_w_out)
    ffn2 = (bf(l2_ffn_w_gate), bf(l2_ffn_w_up), bf(l2_ffn_w_down))
    w_out3 = bf(l3_attn_w_out)
    moe3 = (bf(l3_moe_w_gate), bf(l3_moe_w_up), bf(l3_moe_w_down))
    b_in0 = l0_conv_b_in.reshape(1, -1)
    b_in2 = l2_rec_b_in.reshape(1, -1)

    def trunk(x3):
        batch, seq, _ = x3.shape
        x = x3.reshape(batch * seq, d)
        glu = _mm_pair(x, w_in0, b_in0, mode="glu")
        x = _conv_out_ln(glu, x, l0_conv_w_dw, l0_conv_b_dw, l0_conv_norm_g, l0_conv_norm_b,
                         w_out0, l0_conv_b_out, l0_ln1_g, l0_ln1_b, seq)
        x = _ffn_ln(x, *ffn0, l0_ln2_g, l0_ln2_b)
        qkv = _mm(x, w_qkv1, BF16)
        o = _diff_attn(qkv, diff_scalars, l1_attn_subln_g, batch, seq, n_diff, 1.0 - lambda_init)
        x = _proj_ln([o], x, w_out1, None, l1_ln1_g, l1_ln1_b)
        x = _moe_ln(x, l1_moe_w_router, *moe1, l1_ln2_g, l1_ln2_b)
        y, r = _mm_pair(x, w_in2, b_in2, mode="gelu")
        r = r.reshape(batch, seq, lru_w)
        hs = [_lru_dir(r, l2_rec_conv_w, l2_rec_conv_b, gate_w[dr], l2_rec_gate_b[dr], lru_c[dr:dr + 1],
                       reverse=bool(dr), left=lru_left).reshape(batch * seq, lru_w) for dr in range(2)]
        x = _proj_ln([hs[0], hs[1], y], x, w_out2, l2_rec_b_out, l2_ln1_g, l2_ln1_b, mode="lru")
        x = _ffn_ln(x, *ffn2, l2_ln2_g, l2_ln2_b)
        qkv = _mm(x, w_qkv3, BF16)
        o = _win_attn(qkv, swa_scalars, batch, seq, n_swa)
        x = _proj_ln([o], x, w_out3, None, l3_ln1_g, l3_ln1_b)
        x = _moe_ln(x, l3_moe_w_router, *moe3, l3_ln2_g, l3_ln2_b)
        return x.reshape(batch, seq, d)

    return (trunk(x_prompt), trunk(x_sample))
```

```python
import functools
import math

import jax
import jax.numpy as jnp
from jax import lax
from jax.experimental import pallas as pl
from jax.experimental.pallas import tpu as pltpu

F32 = jnp.float32
BF16 = jnp.bfloat16
I32 = jnp.int32

N_LAYERS = 4
LN_EPS = 1e-5
SUBLN_EPS = 1e-5
ALPHA = (2 * N_LAYERS) ** 0.25
LRU_C = 8.0
DIFF_HEAD_DIM = 128
SWA_HEAD_DIM = 128
SWA_KV_HEADS = 4
SWA_BLOCK = 128
TOP_K = 2
LOG2E = math.log2(math.e)
LANES = 128
HALO_ROWS = 16
LRU_HALO = 8
MIB = 1024 * 1024


def _cparams(vmem_mib, n_axes):
    return pltpu.CompilerParams(
        dimension_semantics=("arbitrary",) * n_axes, vmem_limit_bytes=vmem_mib * MIB)


def _tile(n, pref):
    t = min(n, pref)
    while n % t:
        t //= 2
    return t


def _sigmoid(x):
    return 0.5 * jnp.tanh(0.5 * x) + 0.5


def _silu(x):
    return x * _sigmoid(x)


def _gelu_tanh(x):
    return 0.5 * x * (1.0 + jnp.tanh(math.sqrt(2.0 / math.pi) * (x + 0.044715 * (x * x * x))))


def _dot(a, b):
    return jnp.dot(a, b, preferred_element_type=F32)


def _dot_nt(a, b):
    return lax.dot_general(a, b, (((1,), (1,)), ((), ())), preferred_element_type=F32)


def _layer_norm(y, g, b):
    mu = jnp.mean(y, axis=-1, keepdims=True)
    yc = y - mu
    var = jnp.mean(yc * yc, axis=-1, keepdims=True)
    return yc * lax.rsqrt(var + LN_EPS) * g + b


def _mm_body(x_ref, w_ref, o_ref, xb):
    @pl.when(pl.program_id(1) == 0)
    def _():
        xb[...] = x_ref[...].astype(BF16)

    o_ref[...] = _dot(xb[...], w_ref[...]).astype(o_ref.dtype)


def _mm(x, w, out_dtype):
    m, k = x.shape
    n = w.shape[1]
    tm, tn = _tile(m, 512), _tile(n, 2048)
    return pl.pallas_call(
        _mm_body,
        grid=(m // tm, n // tn),
        in_specs=[pl.BlockSpec((tm, k), lambda i, j: (i, 0)),
                  pl.BlockSpec((k, tn), lambda i, j: (0, j))],
        out_specs=pl.BlockSpec((tm, tn), lambda i, j: (i, j)),
        out_shape=jax.ShapeDtypeStruct((m, n), out_dtype),
        scratch_shapes=[pltpu.VMEM((tm, k), BF16)],
        compiler_params=_cparams(48, 2),
        name="mm",
    )(x, w)


def _mm_pair_body(x_ref, wa_ref, wg_ref, ba_ref, bg_ref, *rest, mode):
    xb = rest[-1]

    @pl.when(pl.program_id(1) == 0)
    def _():
        xb[...] = x_ref[...].astype(BF16)

    a = _dot(xb[...], wa_ref[...]) + ba_ref[...]
    g = _dot(xb[...], wg_ref[...]) + bg_ref[...]
    if mode == "glu":
        rest[0][...] = a * _sigmoid(g)
    else:
        rest[0][...] = _gelu_tanh(a).astype(rest[0].dtype)
        rest[1][...] = g


def _mm_pair(x, w, bias, *, mode):
    m, k = x.shape
    n = w.shape[1] // 2
    tm, tn = _tile(m, 512), _tile(n, 1024)
    nj = n // tn
    out_spec = pl.BlockSpec((tm, tn), lambda i, j: (i, j))
    if mode == "glu":
        out_specs, out_shape = out_spec, jax.ShapeDtypeStruct((m, n), F32)
    else:
        out_specs = [out_spec, out_spec]
        out_shape = [jax.ShapeDtypeStruct((m, n), BF16), jax.ShapeDtypeStruct((m, n), F32)]
    return pl.pallas_call(
        functools.partial(_mm_pair_body, mode=mode),
        grid=(m // tm, nj),
        in_specs=[pl.BlockSpec((tm, k), lambda i, j: (i, 0)),
                  pl.BlockSpec((k, tn), lambda i, j: (0, j)),
                  pl.BlockSpec((k, tn), lambda i, j: (0, j + nj)),
                  pl.BlockSpec((1, tn), lambda i, j: (0, j)),
                  pl.BlockSpec((1, tn), lambda i, j: (0, j + nj))],
        out_specs=out_specs,
        out_shape=out_shape,
        scratch_shapes=[pltpu.VMEM((tm, k), BF16)],
        compiler_params=_cparams(48, 2),
        name="mm_" + mode,
    )(x, w, w, bias, bias)


def _conv_out_body(cur_ref, prv_ref, nxt_ref, res_ref, wdw_ref, bdw_ref, ng_ref, nb_ref,
                   w_ref, bo_ref, g_ref, b_ref, o_ref, ext, cslab, lhs, *, seq, ts, width):
    i = pl.program_id(0)
    t0 = i * ts
    first = (t0 % seq) == 0
    last = ((t0 + ts) % seq) == 0
    nsl = cur_ref.shape[1] // LANES
    half = (width - 1) // 2
    for c in range(nsl):
        cs = slice(c * LANES, (c + 1) * LANES)
        ext[c, 0:HALO_ROWS, :] = jnp.where(first, 0.0, prv_ref[:, cs])
        ext[c, HALO_ROWS:HALO_ROWS + ts, :] = cur_ref[:, cs]
        ext[c, HALO_ROWS + ts:, :] = jnp.where(last, 0.0, nxt_ref[:, cs])

    def slab(c, carry):
        wk = wdw_ref[c]
        acc = jnp.broadcast_to(bdw_ref[c], (ts, LANES))
        for k in range(width):
            acc = acc + wk[k:k + 1, :] * ext[c, pl.ds(HALO_ROWS - half + k, ts), :]
        cslab[c] = acc
        return carry

    lax.fori_loop(0, nsl, slab, 0)

    s1 = cslab[0]
    for c in range(1, nsl):
        s1 = s1 + cslab[c]
    mu = jnp.sum(s1, axis=-1, keepdims=True) * (1.0 / (nsl * LANES))
    s2 = jnp.square(cslab[0] - mu)
    for c in range(1, nsl):
        s2 = s2 + jnp.square(cslab[c] - mu)
    var = jnp.sum(s2, axis=-1, keepdims=True) * (1.0 / (nsl * LANES))
    rstd = lax.rsqrt(var + LN_EPS)
    for c in range(nsl):
        cs = slice(c * LANES, (c + 1) * LANES)
        v = (cslab[c] - mu) * rstd * ng_ref[:, cs] + nb_ref[:, cs]
        lhs[:, cs] = _silu(v).astype(BF16)
    h = _dot(lhs[...], w_ref[...]) + bo_ref[...]
    o_ref[...] = _layer_norm(ALPHA * res_ref[...] + h, g_ref[...], b_ref[...])


def _conv_out_ln(glu, res, w_dw, b_dw, norm_g, norm_b, w_out, b_out, ln_g, ln_b, seq):
    t, d = glu.shape
    width = w_dw.shape[0]
    assert (width - 1) // 2 <= HALO_ROWS and d % LANES == 0
    ts = _tile(seq, 256)
    nsl = d // LANES
    hb = ts // HALO_ROWS
    wdw = w_dw.reshape(width, nsl, LANES).transpose(1, 0, 2)
    bdw = b_dw.reshape(nsl, 1, LANES)
    row = lambda v: v.reshape(1, d)
    vec = pl.BlockSpec((1, d), lambda i: (0, 0))
    return pl.pallas_call(
        functools.partial(_conv_out_body, seq=seq, ts=ts, width=width),
        grid=(t // ts,),
        in_specs=[pl.BlockSpec((ts, d), lambda i: (i, 0)),
                  pl.BlockSpec((HALO_ROWS, d), lambda i: (jnp.maximum(i * hb - 1, 0), 0)),
                  pl.BlockSpec((HALO_ROWS, d), lambda i: (jnp.minimum((i + 1) * hb, t // HALO_ROWS - 1), 0)),
                  pl.BlockSpec((ts, d), lambda i: (i, 0)),
                  pl.BlockSpec((nsl, width, LANES), lambda i: (0, 0, 0)),
                  pl.BlockSpec((nsl, 1, LANES), lambda i: (0, 0, 0)),
                  vec, vec,
                  pl.BlockSpec((d, d), lambda i: (0, 0)),
                  vec, vec, vec],
        out_specs=pl.BlockSpec((ts, d), lambda i: (i, 0)),
        out_shape=jax.ShapeDtypeStruct((t, d), F32),
        scratch_shapes=[pltpu.VMEM((nsl, ts + 2 * HALO_ROWS, LANES), F32),
                        pltpu.VMEM((nsl, ts, LANES), F32),
                        pltpu.VMEM((ts, d), BF16)],
        compiler_params=_cparams(48, 1),
        name="conv_out_ln",
    )(glu, glu, glu, res, wdw, bdw, row(norm_g), row(norm_b), w_out, row(b_out), row(ln_g), row(ln_b))


def _proj_ln_body(*refs, mode, has_bias):
    if mode == "lru":
        hf_ref, hb_ref, y_ref = refs[:3]
        rest = refs[3:]
        lhs = ((hf_ref[...] + hb_ref[...]) * y_ref[...].astype(F32)).astype(BF16)
    else:
        rest = refs[1:]
        lhs = refs[0][...].astype(BF16)
    if has_bias:
        res_ref, w_ref, bo_ref, g_ref, b_ref, o_ref = rest
    else:
        res_ref, w_ref, g_ref, b_ref, o_ref = rest
    h = _dot(lhs, w_ref[...])
    if has_bias:
        h = h + bo_ref[...]
    o_ref[...] = _layer_norm(ALPHA * res_ref[...] + h, g_ref[...], b_ref[...])


def _proj_ln(lhs_args, res, w, b_out, ln_g, ln_b, *, mode="plain"):
    t, d = res.shape
    k = w.shape[0]
    tm = _tile(t, 512 if mode == "plain" else 256)
    tok = lambda n: pl.BlockSpec((tm, n), lambda i: (i, 0))
    vec = pl.BlockSpec((1, d), lambda i: (0, 0))
    row = lambda v: v.reshape(1, d)
    in_specs = [tok(k) for _ in lhs_args] + [tok(d), pl.BlockSpec((k, d), lambda i: (0, 0))]
    args = list(lhs_args) + [res, w]
    if b_out is not None:
        in_specs.append(vec)
        args.append(row(b_out))
    in_specs += [vec, vec]
    args += [row(ln_g), row(ln_b)]
    return pl.pallas_call(
        functools.partial(_proj_ln_body, mode=mode, has_bias=b_out is not None),
        grid=(t // tm,),
        in_specs=in_specs,
        out_specs=tok(d),
        out_shape=jax.ShapeDtypeStruct((t, d), F32),
        compiler_params=_cparams(48, 1),
        name="proj_ln_" + mode,
    )(*args)


def _ffn_body(x_ref, wg_ref, wu_ref, wd_ref, g_ref, b_ref, o_ref, xb, acc):
    f = pl.program_id(1)

    @pl.when(f == 0)
    def _():
        xb[...] = x_ref[...].astype(BF16)
        acc[...] = jnp.zeros_like(acc)

    g = _dot(xb[...], wg_ref[...])
    u = _dot(xb[...], wu_ref[...])
    acc[...] += _dot((_silu(g) * u).astype(BF16), wd_ref[...])

    @pl.when(f == pl.num_programs(1) - 1)
    def _():
        o_ref[...] = _layer_norm(ALPHA * x_ref[...] + acc[...], g_ref[...], b_ref[...])


def _ffn_ln(x, w_gate, w_up, w_down, ln_g, ln_b):
    t, d = x.shape
    fdim = w_gate.shape[1]
    tm, tf = _tile(t, 512), _tile(fdim, 1024)
    vec = pl.BlockSpec((1, d), lambda i, f: (0, 0))
    return pl.pallas_call(
        _ffn_body,
        grid=(t // tm, fdim // tf),
        in_specs=[pl.BlockSpec((tm, d), lambda i, f: (i, 0)),
                  pl.BlockSpec((d, tf), lambda i, f: (0, f)),
                  pl.BlockSpec((d, tf), lambda i, f: (0, f)),
                  pl.BlockSpec((tf, d), lambda i, f: (f, 0)),
                  vec, vec],
        out_specs=pl.BlockSpec((tm, d), lambda i, f: (i, 0)),
        out_shape=jax.ShapeDtypeStruct((t, d), F32),
        scratch_shapes=[pltpu.VMEM((tm, d), BF16), pltpu.VMEM((tm, d), F32)],
        compiler_params=_cparams(56, 2),
        name="ffn_ln",
    )(x, w_gate, w_up, w_down, ln_g.reshape(1, d), ln_b.reshape(1, d))


DIFF_KEY_RADIX = 256
DIFF_SC_PER_HEAD = 7


def _split3_f32(x):
    hi = x.astype(BF16).astype(F32)
    mid = (x - hi).astype(BF16).astype(F32)
    lo = (x - hi - mid).astype(BF16).astype(F32)
    return hi, mid, lo


def _lane_tiles(x, op):
    return functools.reduce(op, [x[:, t * LANES:(t + 1) * LANES] for t in range(x.shape[1] // LANES)])


def _diff_attn_body(sc_ref, q_ref, k_ref, v_ref, kx_ref, sg_ref, o_ref, qa, sfull, oacc, m128, l128,
                    *, seq, tq, out_scale):
    hd = DIFF_HEAD_DIM
    tk = tq
    nk = seq // tk
    h = pl.program_id(1)
    i = pl.program_id(2)
    i0 = i * tq
    lam = sc_ref[0]
    base = 1 + DIFF_SC_PER_HEAD * h
    slope2 = sc_ref[base + 6]

    irel = lax.broadcasted_iota(I32, (tq, 1), 0).astype(F32)
    lane = lax.broadcasted_iota(I32, (tq, hd), 1)
    ext = jnp.zeros((tq, hd), F32)
    for j, v in enumerate((*_split3_f32(-slope2 * irel), *[sc_ref[base + t] for t in range(6)])):
        ext = jnp.where(lane == j, v, ext)
    for c in range(2):
        qc = q_ref[:, c * hd:(c + 1) * hd]
        qa[c, 0] = jnp.concatenate([qc, ext.astype(BF16)], axis=1)
        qa[c, 1] = jnp.concatenate([qc, (-ext).astype(BF16)], axis=1)

    def tile(r):
        kt = lax.rem(i + r, nk)
        j0 = pl.multiple_of(kt * tk, tk)
        if r == 0:
            return j0, 0, 0.0
        c0 = (i0 - j0).astype(F32)
        right = kt > i
        return j0, right.astype(I32), jnp.where(right, slope2 * c0, -slope2 * c0)

    d = (lax.broadcasted_iota(I32, (tq, tk), 0) - lax.broadcasted_iota(I32, (tq, tk), 1)).astype(F32)
    fix = (2.0 * slope2) * jnp.minimum(d, 0.0)
    m128[...] = jnp.full_like(m128, -jnp.inf)
    for r in range(nk):
        j0, sgn, kappa = tile(r)
        kk = k_ref[pl.ds(j0, tk), :]
        for c in range(2):
            ka = jnp.concatenate([kk[:, c * hd:(c + 1) * hd], kx_ref[...]], axis=1)
            s = _dot_nt(qa[c, sgn], ka)
            if r == 0:
                s = s + fix
            sfull[c, :, r * tk:(r + 1) * tk] = s
            m128[c] = jnp.maximum(m128[c], _lane_tiles(s, jnp.maximum) + kappa)

    mrow = [jnp.max(m128[c], axis=-1, keepdims=True) for c in range(2)]
    l128[...] = jnp.zeros_like(l128)
    oacc[...] = jnp.zeros_like(oacc)
    for r in range(nk):
        j0, _, kappa = tile(r)
        vv = v_ref[pl.ds(j0, tk), :]
        for c in range(2):
            p = jnp.exp2(sfull[c, :, r * tk:(r + 1) * tk] - (mrow[c] - kappa))
            l128[c] += _lane_tiles(p, jnp.add)
            oacc[c] += _dot(p.astype(BF16), vv)
    lsum = [jnp.sum(l128[c], axis=-1, keepdims=True) for c in range(2)]
    o = oacc[0] / lsum[0] - lam * (oacc[1] / lsum[1])
    o = o * lax.rsqrt(jnp.mean(o * o, axis=-1, keepdims=True) + SUBLN_EPS) * (sg_ref[...] * out_scale)
    o_ref[...] = o.astype(o_ref.dtype)


def _diff_attn(qkv, scalars, subln_g, batch, seq, n_heads, out_scale):
    t = qkv.shape[0]
    hd = DIFF_HEAD_DIM
    vd = 2 * hd
    tq = _tile(seq, 512)
    assert tq <= DIFF_KEY_RADIX * DIFF_KEY_RADIX
    nq = seq // tq
    lane = jnp.arange(hd)[None, :]
    jrel = jnp.arange(tq)[:, None]
    jl = (jrel % DIFF_KEY_RADIX).astype(F32)
    jh = (jrel // DIFF_KEY_RADIX).astype(F32)
    kx = jnp.where(lane < 3, 1.0, jnp.where(lane < 6, jl, jnp.where(lane < 9, jh, 0.0))).astype(BF16)
    return pl.pallas_call(
        functools.partial(_diff_attn_body, seq=seq, tq=tq, out_scale=out_scale),
        grid=(batch, n_heads, nq),
        in_specs=[pl.BlockSpec(memory_space=pltpu.SMEM),
                  pl.BlockSpec((tq, vd), lambda b, h, i: (b * nq + i, h)),
                  pl.BlockSpec((seq, vd), lambda b, h, i: (b, n_heads + h)),
                  pl.BlockSpec((seq, vd), lambda b, h, i: (b, 2 * n_heads + h)),
                  pl.BlockSpec((tq, hd), lambda b, h, i: (0, 0)),
                  pl.BlockSpec((1, vd), lambda b, h, i: (0, 0))],
        out_specs=pl.BlockSpec((tq, vd), lambda b, h, i: (b * nq + i, h)),
        out_shape=jax.ShapeDtypeStruct((t, n_heads * vd), BF16),
        scratch_shapes=[pltpu.VMEM((2, 2, tq, vd), BF16), pltpu.VMEM((2, tq, seq), F32),
                        pltpu.VMEM((2, tq, vd), F32), pltpu.VMEM((2, tq, LANES), F32),
                        pltpu.VMEM((2, tq, LANES), F32)],
        compiler_params=_cparams(48, 3),
        name="diff_attn",
    )(scalars, qkv, qkv, qkv, kx, subln_g.reshape(1, vd))


def _win_attn_body(sc_ref, q_ref, kp_ref, kc_ref, kn_ref, vp_ref, vc_ref, vn_ref, o_ref,
                   *, nb, n_heads):
    d, blk = SWA_HEAD_DIM, SWA_BLOCK
    n = pl.program_id(1)
    grp = n_heads // SWA_KV_HEADS
    rows = grp * blk
    r = lax.broadcasted_iota(I32, (rows, 3 * blk), 0) % blk
    c = lax.broadcasted_iota(I32, (rows, 3 * blk), 1)
    rel = jnp.abs(r - c + blk)
    valid = (rel <= blk) & ((c >= blk) | (n > 0)) & ((c < 2 * blk) | (n < nb - 1))
    relf = rel.astype(F32)
    mask = jnp.where(valid, 0.0, -jnp.inf)
    head_of_row = lax.broadcasted_iota(I32, (rows, 1), 0) // blk
    for kv in range(SWA_KV_HEADS):
        cs = slice(kv * d, (kv + 1) * d)
        kw = jnp.concatenate([kp_ref[:, cs], kc_ref[:, cs], kn_ref[:, cs]], axis=0)
        vw = jnp.concatenate([vp_ref[:, cs], vc_ref[:, cs], vn_ref[:, cs]], axis=0)
        qg = jnp.concatenate([q_ref[:, (kv * grp + g) * d:(kv * grp + g + 1) * d] for g in range(grp)], axis=0)
        slope = jnp.zeros((rows, 1), F32)
        sink = jnp.zeros((rows, 1), F32)
        for g in range(grp):
            slope = jnp.where(head_of_row == g, sc_ref[kv * grp + g], slope)
            sink = jnp.where(head_of_row == g, sc_ref[n_heads + kv * grp + g], sink)
        s = _dot_nt(qg, kw) - slope * relf + mask
        m = jnp.maximum(jnp.max(s, axis=-1, keepdims=True), sink)
        p = jnp.exp2(s - m)
        denom = jnp.sum(p, axis=-1, keepdims=True) + jnp.exp2(sink - m)
        o = _dot(p.astype(BF16), vw) / denom
        for g in range(grp):
            hcol = (kv * grp + g) * d
            o_ref[:, hcol:hcol + d] = o[g * blk:(g + 1) * blk, :].astype(o_ref.dtype)


def _win_attn(qkv, scalars, batch, seq, n_heads):
    t = qkv.shape[0]
    d, blk = SWA_HEAD_DIM, SWA_BLOCK
    nb = seq // blk
    qw, kvw = n_heads * d, SWA_KV_HEADS * d
    assert qw % kvw == 0
    kcol, vcol = qw // kvw, qw // kvw + 1
    prev = lambda b, n: b * nb + jnp.maximum(n - 1, 0)
    nxt = lambda b, n: b * nb + jnp.minimum(n + 1, nb - 1)
    cur = lambda b, n: b * nb + n
    kvspec = lambda f, col: pl.BlockSpec((blk, kvw), lambda b, n: (f(b, n), col))
    return pl.pallas_call(
        functools.partial(_win_attn_body, nb=nb, n_heads=n_heads),
        grid=(batch, nb),
        in_specs=[pl.BlockSpec(memory_space=pltpu.SMEM),
                  pl.BlockSpec((blk, qw), lambda b, n: (cur(b, n), 0)),
                  kvspec(prev, kcol), kvspec(cur, kcol), kvspec(nxt, kcol),
                  kvspec(prev, vcol), kvspec(cur, vcol), kvspec(nxt, vcol)],
        out_specs=pl.BlockSpec((blk, qw), lambda b, n: (cur(b, n), 0)),
        out_shape=jax.ShapeDtypeStruct((t, qw), BF16),
        compiler_params=_cparams(32, 2),
        name="win_attn",
    )(scalars, qkv, qkv, qkv, qkv, qkv, qkv, qkv)


def _lru_body(r_hbm, cw_ref, cb_ref, gw_ref, gb_ref, c_ref, h_hbm,
              rbuf, hout, abuf, bbuf, hcar, isem, osem, *, seq, ts, batch, reverse, left):
    i = pl.program_id(0)
    nch = seq // ts
    taps = cw_ref.shape[0]
    right = taps - 1 - left
    hal = LRU_HALO
    width = r_hbm.shape[2]
    nblk, bw = gw_ref.shape[1], gw_ref.shape[2]
    chunk = lambda s: (nch - 1 - s) if reverse else s
    slot = i % 2

    def in_copies(ci, sl, op):
        t0 = ci * ts

        def each(src_start, dst_start, rows):
            src_start = pl.multiple_of(src_start, LRU_HALO)
            for b in range(batch):
                cp = pltpu.make_async_copy(r_hbm.at[b, pl.ds(src_start, rows), :],
                                           rbuf.at[sl, pl.ds(dst_start, rows), b, :], isem.at[sl])
                getattr(cp, op)()

        each(t0, hal, ts)
        pl.when(ci > 0)(lambda: each(jnp.maximum(t0 - hal, 0), 0, hal))
        pl.when(ci < nch - 1)(lambda: each(jnp.minimum(t0 + ts, seq - hal), hal + ts, hal))

    def out_copies(ci, sl, op):
        for b in range(batch):
            cp = pltpu.make_async_copy(hout.at[sl, pl.ds(0, ts), b, :],
                                       h_hbm.at[b, pl.ds(ci * ts, ts), :], osem.at[sl])
            getattr(cp, op)()

    @pl.when(i == 0)
    def _():
        hcar[...] = jnp.zeros_like(hcar)
        in_copies(chunk(0), 0, "start")

    @pl.when(i + 1 < nch)
    def _():
        in_copies(chunk(i + 1), 1 - slot, "start")

    ci = chunk(i)
    in_copies(ci, slot, "wait")

    @pl.when(ci == 0)
    def _():
        rbuf[slot, hal - left:hal] = jnp.zeros((left, batch, width), F32)

    @pl.when(ci == nch - 1)
    def _():
        rbuf[slot, hal + ts:hal + ts + right] = jnp.zeros((right, batch, width), F32)

    rf = jnp.broadcast_to(cb_ref[...], (ts, batch, width))
    for k in range(taps):
        rf = rf + cw_ref[k:k + 1, :] * rbuf[slot, pl.ds(hal - left + k, ts)]
    rf2 = rf.reshape(ts * batch, width)
    for nbk in range(nblk):
        cs = slice(nbk * bw, (nbk + 1) * bw)
        xb = rf2[:, cs]
        xbb = xb.astype(BF16)
        rg = _sigmoid(_dot(xbb, gw_ref[0, nbk]) + gb_ref[0:1, cs])
        ig = _sigmoid(_dot(xbb, gw_ref[1, nbk]) + gb_ref[1:2, cs])
        a = jnp.exp(c_ref[:, cs] * rg)
        bb = jnp.sqrt(1.0 - a * a) * (ig * xb)
        abuf[:, :, cs] = a.reshape(ts, batch, bw)
        bbuf[:, :, cs] = bb.reshape(ts, batch, bw)

    @pl.when(i >= 2)
    def _():
        out_copies(ci, slot, "wait")

    def step(s, h):
        t = (ts - 1 - s) if reverse else s
        h = abuf[t] * h + bbuf[t]
        hout[slot, t] = h
        return h

    hcar[...] = lax.fori_loop(0, ts, step, hcar[...])
    out_copies(ci, slot, "start")

    @pl.when(i == nch - 1)
    def _():
        out_copies(ci, slot, "wait")
        if nch >= 2:
            out_copies(ci, 1 - slot, "wait")


def _lru_dir(r3, conv_w, conv_b, gate_w, gate_b, cvec, *, reverse, left):
    batch, seq, width = r3.shape
    ts = _tile(seq, 512 // batch)
    taps = conv_w.shape[0]
    assert batch % 8 == 0 and ts % LRU_HALO == 0 and max(left, taps - 1 - left) <= LRU_HALO
    full = lambda a: pl.BlockSpec(a.shape, lambda i: (0,) * a.ndim)
    cb = conv_b.reshape(1, width)
    return pl.pallas_call(
        functools.partial(_lru_body, seq=seq, ts=ts, batch=batch, reverse=reverse, left=left),
        grid=(seq // ts,),
        in_specs=[pl.BlockSpec(memory_space=pl.ANY), full(conv_w), full(cb), full(gate_w),
                  full(gate_b), full(cvec)],
        out_specs=pl.BlockSpec(memory_space=pl.ANY),
        out_shape=jax.ShapeDtypeStruct((batch, seq, width), F32),
        scratch_shapes=[pltpu.VMEM((2, ts + 2 * LRU_HALO, batch, width), F32),
                        pltpu.VMEM((2, ts, batch, width), F32),
                        pltpu.VMEM((ts, batch, width), F32),
                        pltpu.VMEM((ts, batch, width), F32),
                        pltpu.VMEM((batch, width), F32),
                        pltpu.SemaphoreType.DMA((2,)),
                        pltpu.SemaphoreType.DMA((2,))],
        compiler_params=_cparams(48, 1),
        name="lru_rev" if reverse else "lru_fwd",
    )(r3, conv_w, cb, gate_w, gate_b, cvec)


def _split_bf16(x):
    hi = x.astype(BF16)
    lo = (x - hi.astype(F32)).astype(BF16)
    return hi, lo


def _router_body(x_ref, wr_ref, o_ref, cnt_ref, cnt, lower):
    i = pl.program_id(0)
    tm = x_ref.shape[0]
    ne = wr_ref.shape[1]

    @pl.when(i == 0)
    def _():
        cnt[...] = jnp.zeros_like(cnt)
        lower[...] = jnp.where(lax.broadcasted_iota(I32, (tm, tm), 1) < lax.broadcasted_iota(I32, (tm, tm), 0),
                               1.0, 0.0).astype(BF16)

    xh, xl = _split_bf16(x_ref[...])
    wh, wl = _split_bf16(wr_ref[...])
    logits = _dot(xh, wh) + _dot(xl, wh) + _dot(xh, wl)
    lane = lax.broadcasted_iota(I32, (tm, ne), 1).astype(F32)
    v1 = jnp.max(logits, axis=-1, keepdims=True)
    i1 = jnp.min(jnp.where(logits == v1, lane, float(ne)), axis=-1, keepdims=True)
    rest = jnp.where(lane == i1, -jnp.inf, logits)
    v2 = jnp.max(rest, axis=-1, keepdims=True)
    i2 = jnp.min(jnp.where(rest == v2, lane, float(ne)), axis=-1, keepdims=True)
    e2 = jnp.exp(v2 - v1)
    w1 = 1.0 / (1.0 + e2)
    w2 = e2 / (1.0 + e2)
    oh1 = lane == i1
    oh2 = lane == i2
    oh = jnp.where(oh1 | oh2, 1.0, 0.0)
    before = _dot(lower[...], oh.astype(BF16)) + cnt[...]
    r1 = jnp.sum(jnp.where(oh1, before, 0.0), axis=-1, keepdims=True)
    r2 = jnp.sum(jnp.where(oh2, before, 0.0), axis=-1, keepdims=True)
    cnt[...] += jnp.sum(oh, axis=0, keepdims=True)
    cnt_ref[...] = cnt[...]
    col = lax.broadcasted_iota(I32, (tm, 8), 1)
    out = jnp.zeros((tm, 8), F32)
    for j, v in enumerate((i1, i2, w1, w2, r1, r2)):
        out = jnp.where(col == j, v, out)
    o_ref[...] = out


def _router(x, w_router):
    t, d = x.shape
    ne = w_router.shape[1]
    tm = _tile(t, 512)
    return pl.pallas_call(
        _router_body,
        grid=(t // tm,),
        in_specs=[pl.BlockSpec((tm, d), lambda i: (i, 0)),
                  pl.BlockSpec((d, ne), lambda i: (0, 0))],
        out_specs=[pl.BlockSpec((tm, 8), lambda i: (i, 0)),
                   pl.BlockSpec((1, ne), lambda i: (0, 0))],
        out_shape=[jax.ShapeDtypeStruct((t, 8), F32), jax.ShapeDtypeStruct((1, ne), F32)],
        scratch_shapes=[pltpu.VMEM((1, ne), F32), pltpu.VMEM((tm, tm), BF16)],
        compiler_params=_cparams(32, 1),
        name="router",
    )(x, w_router)


ROW_DMA_UNROLL = 8


def _start_row_gather(idx_ref, n_rows, src_hbm, dst, sem):
    def issue(r, carry):
        row = idx_ref[0, 0, r]
        pltpu.make_async_copy(src_hbm.at[pl.ds(row, 1), :], dst.at[pl.ds(r, 1), :], sem).start()
        return carry
    lax.fori_loop(0, n_rows, issue, 0, unroll=ROW_DMA_UNROLL)


def _wait_rows(n_rows, hbm, vmem, sem):
    pltpu.make_async_copy(hbm.at[pl.ds(0, n_rows), :], vmem, sem).wait()


def _dispatch_body(off_ref, pad_ref, nv_ref, pos_ref, x_ref, xs_hbm, zbuf, sem):
    i = pl.program_id(0)
    tm = x_ref.shape[0]
    tz = zbuf.shape[0]

    @pl.when(i == 0)
    def _():
        zbuf[...] = jnp.zeros_like(zbuf)

        def zero_tile(row0):
            row0 = pl.multiple_of(row0, tz)
            cp = pltpu.make_async_copy(zbuf, xs_hbm.at[pl.ds(row0, tz), :], sem.at[1])
            cp.start()
            cp.wait()

        for e in range(off_ref.shape[0]):
            pl.when(pad_ref[e] > 0)(functools.partial(zero_tile, off_ref[e] + pad_ref[e] - tz))

        def tail(j, carry):
            zero_tile(j * tz)
            return carry
        lax.fori_loop(nv_ref[0], xs_hbm.shape[0] // tz, tail, 0)

    def issue(r, carry):
        for slot in range(TOP_K):
            dst = pos_ref[0, 0, slot * tm + r]
            pltpu.make_async_copy(x_ref.at[pl.ds(r, 1), :], xs_hbm.at[pl.ds(dst, 1), :], sem.at[0]).start()
        return carry

    lax.fori_loop(0, tm, issue, 0, unroll=ROW_DMA_UNROLL)
    for _ in range(TOP_K):
        _wait_rows(tm, xs_hbm, x_ref, sem.at[0])


def _dispatch(x, pos_tiles, offsets, padded, n_valid, n_rows, tm, tm_e):
    t, d = x.shape
    grid_spec = pltpu.PrefetchScalarGridSpec(
        num_scalar_prefetch=3,
        grid=(t // tm,),
        in_specs=[pl.BlockSpec((1, 1, TOP_K * tm), lambda i, o, p, nv: (i, 0, 0), memory_space=pltpu.SMEM),
                  pl.BlockSpec((tm, d), lambda i, o, p, nv: (i, 0))],
        out_specs=pl.BlockSpec(memory_space=pl.ANY),
        scratch_shapes=[pltpu.VMEM((tm_e, d), F32), pltpu.SemaphoreType.DMA((2,))],
    )
    return pl.pallas_call(
        _dispatch_body,
        grid_spec=grid_spec,
        out_shape=jax.ShapeDtypeStruct((n_rows, d), F32),
        compiler_params=_cparams(32, 1),
        name="moe_dispatch",
    )(offsets, padded, n_valid, pos_tiles, x)


def _expert_body(te_ref, nv_ref, x_ref, wg_ref, wu_ref, wd_ref, o_ref):
    j = pl.program_id(0)

    @pl.when(j < nv_ref[0])
    def _():
        xb = x_ref[...].astype(BF16)
        g = _dot(xb, wg_ref[0])
        u = _dot(xb, wu_ref[0])
        o_ref[...] = _dot((_silu(g) * u).astype(BF16), wd_ref[0])

    @pl.when(j >= nv_ref[0])
    def _():
        o_ref[...] = jnp.zeros_like(o_ref)


def _experts(xs, tile_expert, n_valid, w_gate, w_up, w_down, tm):
    n_rows, d = xs.shape
    fdim = w_gate.shape[2]
    nt = n_rows // tm
    wspec = lambda shape: pl.BlockSpec((1,) + shape, lambda j, te, nv: (te[j], 0, 0))
    grid_spec = pltpu.PrefetchScalarGridSpec(
        num_scalar_prefetch=2,
        grid=(nt,),
        in_specs=[pl.BlockSpec((tm, d), lambda j, te, nv: (jnp.minimum(j, nv[0] - 1), 0)),
                  wspec((d, fdim)), wspec((d, fdim)), wspec((fdim, d))],
        out_specs=pl.BlockSpec((tm, d), lambda j, te, nv: (j, 0)),
    )
    return pl.pallas_call(
        _expert_body,
        grid_spec=grid_spec,
        out_shape=jax.ShapeDtypeStruct((n_rows, d), F32),
        compiler_params=_cparams(56, 1),
        name="experts",
    )(tile_expert, n_valid, xs, w_gate, w_up, w_down)


def _combine_body(pos_cur, pos_nxt, x_ref, route_ref, g_ref, b_ref, ys_hbm, o_ref, buf, sem):
    i = pl.program_id(0)
    n = pl.num_programs(0)
    tm = x_ref.shape[0]
    slot = i % 2

    @pl.when(i == 0)
    def _():
        _start_row_gather(pos_cur, TOP_K * tm, ys_hbm, buf.at[0], sem.at[0])

    @pl.when(i + 1 < n)
    def _():
        _start_row_gather(pos_nxt, TOP_K * tm, ys_hbm, buf.at[1 - slot], sem.at[1 - slot])

    _wait_rows(TOP_K * tm, ys_hbm, buf.at[slot], sem.at[slot])
    f = route_ref[:, 2:3] * buf[slot, 0:tm, :] + route_ref[:, 3:4] * buf[slot, tm:2 * tm, :]
    o_ref[...] = _layer_norm(ALPHA * x_ref[...] + f, g_ref[...], b_ref[...])


def _combine_ln(x, ys, route, pos, ln_g, ln_b, tm):
    t, d = x.shape
    n = t // tm
    pos_spec = lambda f: pl.BlockSpec((1, 1, TOP_K * tm), lambda i: (f(i), 0, 0), memory_space=pltpu.SMEM)
    vec = pl.BlockSpec((1, d), lambda i: (0, 0))
    return pl.pallas_call(
        _combine_body,
        grid=(n,),
        in_specs=[pos_spec(lambda i: i), pos_spec(lambda i: jnp.minimum(i + 1, n - 1)),
                  pl.BlockSpec((tm, d), lambda i: (i, 0)),
                  pl.BlockSpec((tm, route.shape[1]), lambda i: (i, 0)), vec, vec,
                  pl.BlockSpec(memory_space=pl.ANY)],
        out_specs=pl.BlockSpec((tm, d), lambda i: (i, 0)),
        out_shape=jax.ShapeDtypeStruct((t, d), F32),
        scratch_shapes=[pltpu.VMEM((2, TOP_K * tm, d), F32), pltpu.SemaphoreType.DMA((2,))],
        compiler_params=_cparams(40, 1),
        name="moe_combine_ln",
    )(pos, pos, x, route, ln_g.reshape(1, d), ln_b.reshape(1, d), ys)


def _moe_ln(x, w_router, w_gate, w_up, w_down, ln_g, ln_b):
    t, d = x.shape
    ne = w_router.shape[1]
    tm_e = _tile(t, 512)
    tm_c = _tile(t, 256)
    route, counts = _router(x, w_router)
    expert = route[:, 0:2].astype(I32)
    rank = route[:, 4:6].astype(I32)
    counts = counts[0].astype(I32)
    padded = ((counts + tm_e - 1) // tm_e) * tm_e
    ends = jnp.cumsum(padded)
    offsets = ends - padded
    pos = offsets[expert] + rank
    nt = (TOP_K * t) // tm_e + ne
    tile_expert = jnp.minimum(
        jnp.searchsorted(ends, jnp.arange(nt, dtype=I32) * tm_e, side="right"), ne - 1).astype(I32)
    n_valid = (ends[-1] // tm_e).astype(I32).reshape(1)
    tiles = lambda tm: pos.reshape(t // tm, tm, TOP_K).transpose(0, 2, 1).reshape(t // tm, 1, TOP_K * tm)
    xs = _dispatch(x, tiles(tm_c), offsets, padded, n_valid, nt * tm_e, tm_c, tm_e)
    ys = _experts(xs, tile_expert, n_valid, w_gate, w_up, w_down, tm_e)
    return _combine_ln(x, ys, route, tiles(tm_c), ln_g, ln_b, tm_c)


def _alibi_slopes(n_heads):
    return 2.0 ** (-8.0 * jnp.arange(1, n_heads + 1, dtype=F32) / n_heads)


def kernel(x_prompt, x_sample, l0_conv_w_in, l0_conv_b_in, l0_conv_w_dw, l0_conv_b_dw, l0_conv_norm_g, l0_conv_norm_b, l0_conv_w_out, l0_conv_b_out, l0_ln1_g, l0_ln1_b, l0_ffn_w_gate, l0_ffn_w_up, l0_ffn_w_down, l0_ln2_g, l0_ln2_b, l1_attn_w_qkv, l1_attn_lambda, l1_attn_subln_g, l1_attn_w_out, l1_ln1_g, l1_ln1_b, l1_moe_w_router, l1_moe_w_gate, l1_moe_w_up, l1_moe_w_down, l1_ln2_g, l1_ln2_b, l2_rec_w_in, l2_rec_b_in, l2_rec_conv_w, l2_rec_conv_b, l2_rec_gate_w, l2_rec_gate_b, l2_rec_lambda, l2_rec_w_out, l2_rec_b_out, l2_ln1_g, l2_ln1_b, l2_ffn_w_gate, l2_ffn_w_up, l2_ffn_w_down, l2_ln2_g, l2_ln2_b, l3_attn_w_qkv, l3_attn_sink, l3_attn_w_out, l3_ln1_g, l3_ln1_b, l3_moe_w_router, l3_moe_w_gate, l3_moe_w_up, l3_moe_w_down, l3_ln2_g, l3_ln2_b):
    d = x_prompt.shape[-1]
    bf = lambda w: w.astype(BF16)

    diff_w = l1_attn_w_qkv.shape[1] // 3
    n_diff = diff_w // (2 * DIFF_HEAD_DIM)
    col_scale = jnp.concatenate([jnp.full((diff_w,), LOG2E * DIFF_HEAD_DIM ** -0.5, F32), jnp.ones((2 * diff_w,), F32)])
    w_qkv1 = bf(l1_attn_w_qkv * col_scale)
    lam = l1_attn_lambda.astype(F32)
    lambda_init = 0.8 - 0.6 * math.exp(-0.3 * 1)
    lam_full = jnp.exp(jnp.sum(lam[0] * lam[1])) - jnp.exp(jnp.sum(lam[2] * lam[3])) + lambda_init
    slope2 = LOG2E * _alibi_slopes(n_diff)
    diff_scalars = jnp.concatenate(
        [lam_full.reshape(1),
         jnp.stack([*_split3_f32(slope2), *_split3_f32(DIFF_KEY_RADIX * slope2), slope2], axis=1).reshape(-1)])

    lru_w = l2_rec_w_out.shape[0]
    lru_left = l2_rec_conv_w.shape[0] // 2
    lru_c = -LRU_C * jax.nn.softplus(-l2_rec_lambda.astype(F32))
    gate_w = bf(l2_rec_gate_w)

    n_swa = d // SWA_HEAD_DIM
    swa_q = n_swa * SWA_HEAD_DIM
    col_scale3 = jnp.concatenate([jnp.full((swa_q,), LOG2E * SWA_HEAD_DIM ** -0.5, F32),
                                  jnp.ones((l3_attn_w_qkv.shape[1] - swa_q,), F32)])
    w_qkv3 = bf(l3_attn_w_qkv * col_scale3)
    swa_scalars = LOG2E * jnp.concatenate([_alibi_slopes(n_swa), l3_attn_sink.astype(F32)])

    w_in0, w_out0 = bf(l0_conv_w_in), bf(l0_conv_w_out)
    ffn0 = (bf(l0_ffn_w_gate), bf(l0_ffn_w_up), bf(l0_ffn_w_down))
    w_out1 = bf(l1_attn_w_out)
    moe1 = (bf(l1_moe_w_gate), bf(l1_moe_w_up), bf(l1_moe_w_down))
    w_in2, w_out2 = bf(l2_rec_w_in), bf(l2_rec_w_out)
    ffn2 = (bf(l2_ffn_w_gate), bf(l2_ffn_w_up), bf(l2_ffn_w_down))
    w_out3 = bf(l3_attn_w_out)
    moe3 = (bf(l3_moe_w_gate), bf(l3_moe_w_up), bf(l3_moe_w_down))
    b_in0 = l0_conv_b_in.reshape(1, -1)
    b_in2 = l2_rec_b_in.reshape(1, -1)

    def trunk(x3):
        batch, seq, _ = x3.shape
        x = x3.reshape(batch * seq, d)
        glu = _mm_pair(x, w_in0, b_in0, mode="glu")
        x = _conv_out_ln(glu, x, l0_conv_w_dw, l0_conv_b_dw, l0_conv_norm_g, l0_conv_norm_b,
                         w_out0, l0_conv_b_out, l0_ln1_g, l0_ln1_b, seq)
        x = _ffn_ln(x, *ffn0, l0_ln2_g, l0_ln2_b)
        qkv = _mm(x, w_qkv1, BF16)
        o = _diff_attn(qkv, diff_scalars, l1_attn_subln_g, batch, seq, n_diff, 1.0 - lambda_init)
        x = _proj_ln([o], x, w_out1, None, l1_ln1_g, l1_ln1_b)
        x = _moe_ln(x, l1_moe_w_router, *moe1, l1_ln2_g, l1_ln2_b)
        y, r = _mm_pair(x, w_in2, b_in2, mode="gelu")
        r = r.reshape(batch, seq, lru_w)
        hs = [_lru_dir(r, l2_rec_conv_w, l2_rec_conv_b, gate_w[dr], l2_rec_gate_b[dr], lru_c[dr:dr + 1],
                       reverse=bool(dr), left=lru_left).reshape(batch * seq, lru_w) for dr in range(2)]
        x = _proj_ln([hs[0], hs[1], y], x, w_out2, l2_rec_b_out, l2_ln1_g, l2_ln1_b, mode="lru")
        x = _ffn_ln(x, *ffn2, l2_ln2_g, l2_ln2_b)
        qkv = _mm(x, w_qkv3, BF16)
        o = _win_attn(qkv, swa_scalars, batch, seq, n_swa)
        x = _proj_ln([o], x, w_out3, None, l3_ln1_g, l3_ln1_b)
        x = _moe_ln(x, l3_moe_w_router, *moe3, l3_ln2_g, l3_ln2_b)
        return x.reshape(batch, seq, d)

    return (trunk(x_prompt), trunk(x_sample))
```

```python
import functools
import math

import jax
import jax.numpy as jnp
from jax import lax
from jax.experimental import pallas as pl
from jax.experimental.pallas import tpu as pltpu

F32 = jnp.float32
BF16 = jnp.bfloat16
I32 = jnp.int32

N_LAYERS = 4
LN_EPS = 1e-5
SUBLN_EPS = 1e-5
ALPHA = (2 * N_LAYERS) ** 0.25
LRU_C = 8.0
DIFF_HEAD_DIM = 128
SWA_HEAD_DIM = 128
SWA_KV_HEADS = 4
SWA_BLOCK = 128
TOP_K = 2
LOG2E = math.log2(math.e)
LANES = 128
HALO_ROWS = 16
LRU_HALO = 8
MIB = 1024 * 1024


def _cparams(vmem_mib, n_axes):
    return pltpu.CompilerParams(
        dimension_semantics=("arbitrary",) * n_axes, vmem_limit_bytes=vmem_mib * MIB)


def _tile(n, pref):
    t = min(n, pref)
    while n % t:
        t //= 2
    return t


def _sigmoid(x):
    return 0.5 * jnp.tanh(0.5 * x) + 0.5


def _silu(x):
    return x * _sigmoid(x)


def _gelu_tanh(x):
    return 0.5 * x * (1.0 + jnp.tanh(math.sqrt(2.0 / math.pi) * (x + 0.044715 * (x * x * x))))


def _dot(a, b):
    return jnp.dot(a, b, preferred_element_type=F32)


def _dot_nt(a, b):
    return lax.dot_general(a, b, (((1,), (1,)), ((), ())), preferred_element_type=F32)


def _layer_norm(y, g, b):
    mu = jnp.mean(y, axis=-1, keepdims=True)
    yc = y - mu
    var = jnp.mean(yc * yc, axis=-1, keepdims=True)
    return yc * lax.rsqrt(var + LN_EPS) * g + b


def _mm_body(x_ref, w_ref, o_ref, xb):
    @pl.when(pl.program_id(1) == 0)
    def _():
        xb[...] = x_ref[...].astype(BF16)

    o_ref[...] = _dot(xb[...], w_ref[...]).astype(o_ref.dtype)


def _mm(x, w, out_dtype):
    m, k = x.shape
    n = w.shape[1]
    tm, tn = _tile(m, 512), _tile(n, 2048)
    return pl.pallas_call(
        _mm_body,
        grid=(m // tm, n // tn),
        in_specs=[pl.BlockSpec((tm, k), lambda i, j: (i, 0)),
                  pl.BlockSpec((k, tn), lambda i, j: (0, j))],
        out_specs=pl.BlockSpec((tm, tn), lambda i, j: (i, j)),
        out_shape=jax.ShapeDtypeStruct((m, n), out_dtype),
        scratch_shapes=[pltpu.VMEM((tm, k), BF16)],
        compiler_params=_cparams(48, 2),
        name="mm",
    )(x, w)


def _mm_pair_body(x_ref, wa_ref, wg_ref, ba_ref, bg_ref, *rest, mode):
    xb = rest[-1]

    @pl.when(pl.program_id(1) == 0)
    def _():
        xb[...] = x_ref[...].astype(BF16)

    a = _dot(xb[...], wa_ref[...]) + ba_ref[...]
    g = _dot(xb[...], wg_ref[...]) + bg_ref[...]
    if mode == "glu":
        rest[0][...] = a * _sigmoid(g)
    else:
        rest[0][...] = _gelu_tanh(a).astype(rest[0].dtype)
        rest[1][...] = g


def _mm_pair(x, w, bias, *, mode):
    m, k = x.shape
    n = w.shape[1] // 2
    tm, tn = _tile(m, 512), _tile(n, 1024)
    nj = n // tn
    out_spec = pl.BlockSpec((tm, tn), lambda i, j: (i, j))
    if mode == "glu":
        out_specs, out_shape = out_spec, jax.ShapeDtypeStruct((m, n), F32)
    else:
        out_specs = [out_spec, out_spec]
        out_shape = [jax.ShapeDtypeStruct((m, n), BF16), jax.ShapeDtypeStruct((m, n), F32)]
    return pl.pallas_call(
        functools.partial(_mm_pair_body, mode=mode),
        grid=(m // tm, nj),
        in_specs=[pl.BlockSpec((tm, k), lambda i, j: (i, 0)),
                  pl.BlockSpec((k, tn), lambda i, j: (0, j)),
                  pl.BlockSpec((k, tn), lambda i, j: (0, j + nj)),
                  pl.BlockSpec((1, tn), lambda i, j: (0, j)),
                  pl.BlockSpec((1, tn), lambda i, j: (0, j + nj))],
        out_specs=out_specs,
        out_shape=out_shape,
        scratch_shapes=[pltpu.VMEM((tm, k), BF16)],
        compiler_params=_cparams(48, 2),
        name="mm_" + mode,
    )(x, w, w, bias, bias)


def _conv_out_body(cur_ref, prv_ref, nxt_ref, res_ref, wdw_ref, bdw_ref, ng_ref, nb_ref,
                   w_ref, bo_ref, g_ref, b_ref, o_ref, ext, cslab, lhs, *, seq, ts, width):
    i = pl.program_id(0)
    t0 = i * ts
    first = (t0 % seq) == 0
    last = ((t0 + ts) % seq) == 0
    nsl = cur_ref.shape[1] // LANES
    half = (width - 1) // 2
    for c in range(nsl):
        cs = slice(c * LANES, (c + 1) * LANES)
        ext[c, 0:HALO_ROWS, :] = jnp.where(first, 0.0, prv_ref[:, cs])
        ext[c, HALO_ROWS:HALO_ROWS + ts, :] = cur_ref[:, cs]
        ext[c, HALO_ROWS + ts:, :] = jnp.where(last, 0.0, nxt_ref[:, cs])

    def slab(c, carry):
        wk = wdw_ref[c]
        acc = jnp.broadcast_to(bdw_ref[c], (ts, LANES))
        for k in range(width):
            acc = acc + wk[k:k + 1, :] * ext[c, pl.ds(HALO_ROWS - half + k, ts), :]
        cslab[c] = acc
        return carry

    lax.fori_loop(0, nsl, slab, 0)

    s1 = cslab[0]
    for c in range(1, nsl):
        s1 = s1 + cslab[c]
    mu = jnp.sum(s1, axis=-1, keepdims=True) * (1.0 / (nsl * LANES))
    s2 = jnp.square(cslab[0] - mu)
    for c in range(1, nsl):
        s2 = s2 + jnp.square(cslab[c] - mu)
    var = jnp.sum(s2, axis=-1, keepdims=True) * (1.0 / (nsl * LANES))
    rstd = lax.rsqrt(var + LN_EPS)
    for c in range(nsl):
        cs = slice(c * LANES, (c + 1) * LANES)
        v = (cslab[c] - mu) * rstd * ng_ref[:, cs] + nb_ref[:, cs]
        lhs[:, cs] = _silu(v).astype(BF16)
    h = _dot(lhs[...], w_ref[...]) + bo_ref[...]
    o_ref[...] = _layer_norm(ALPHA * res_ref[...] + h, g_ref[...], b_ref[...])


def _conv_out_ln(glu, res, w_dw, b_dw, norm_g, norm_b, w_out, b_out, ln_g, ln_b, seq):
    t, d = glu.shape
    width = w_dw.shape[0]
    assert (width - 1) // 2 <= HALO_ROWS and d % LANES == 0
    ts = _tile(seq, 256)
    nsl = d // LANES
    hb = ts // HALO_ROWS
    wdw = w_dw.reshape(width, nsl, LANES).transpose(1, 0, 2)
    bdw = b_dw.reshape(nsl, 1, LANES)
    row = lambda v: v.reshape(1, d)
    vec = pl.BlockSpec((1, d), lambda i: (0, 0))
    return pl.pallas_call(
        functools.partial(_conv_out_body, seq=seq, ts=ts, width=width),
        grid=(t // ts,),
        in_specs=[pl.BlockSpec((ts, d), lambda i: (i, 0)),
                  pl.BlockSpec((HALO_ROWS, d), lambda i: (jnp.maximum(i * hb - 1, 0), 0)),
                  pl.BlockSpec((HALO_ROWS, d), lambda i: (jnp.minimum((i + 1) * hb, t // HALO_ROWS - 1), 0)),
                  pl.BlockSpec((ts, d), lambda i: (i, 0)),
                  pl.BlockSpec((nsl, width, LANES), lambda i: (0, 0, 0)),
                  pl.BlockSpec((nsl, 1, LANES), lambda i: (0, 0, 0)),
                  vec, vec,
                  pl.BlockSpec((d, d), lambda i: (0, 0)),
                  vec, vec, vec],
        out_specs=pl.BlockSpec((ts, d), lambda i: (i, 0)),
        out_shape=jax.ShapeDtypeStruct((t, d), F32),
        scratch_shapes=[pltpu.VMEM((nsl, ts + 2 * HALO_ROWS, LANES), F32),
                        pltpu.VMEM((nsl, ts, LANES), F32),
                        pltpu.VMEM((ts, d), BF16)],
        compiler_params=_cparams(48, 1),
        name="conv_out_ln",
    )(glu, glu, glu, res, wdw, bdw, row(norm_g), row(norm_b), w_out, row(b_out), row(ln_g), row(ln_b))


def _proj_ln_body(*refs, mode, has_bias):
    if mode == "lru":
        hf_ref, hb_ref, y_ref = refs[:3]
        rest = refs[3:]
        lhs = ((hf_ref[...] + hb_ref[...]) * y_ref[...].astype(F32)).astype(BF16)
    else:
        rest = refs[1:]
        lhs = refs[0][...].astype(BF16)
    if has_bias:
        res_ref, w_ref, bo_ref, g_ref, b_ref, o_ref = rest
    else:
        res_ref, w_ref, g_ref, b_ref, o_ref = rest
    h = _dot(lhs, w_ref[...])
    if has_bias:
        h = h + bo_ref[...]
    o_ref[...] = _layer_norm(ALPHA * res_ref[...] + h, g_ref[...], b_ref[...])


def _proj_ln(lhs_args, res, w, b_out, ln_g, ln_b, *, mode="plain"):
    t, d = res.shape
    k = w.shape[0]
    tm = _tile(t, 512 if mode == "plain" else 256)
    tok = lambda n: pl.BlockSpec((tm, n), lambda i: (i, 0))
    vec = pl.BlockSpec((1, d), lambda i: (0, 0))
    row = lambda v: v.reshape(1, d)
    in_specs = [tok(k) for _ in lhs_args] + [tok(d), pl.BlockSpec((k, d), lambda i: (0, 0))]
    args = list(lhs_args) + [res, w]
    if b_out is not None:
        in_specs.append(vec)
        args.append(row(b_out))
    in_specs += [vec, vec]
    args += [row(ln_g), row(ln_b)]
    return pl.pallas_call(
        functools.partial(_proj_ln_body, mode=mode, has_bias=b_out is not None),
        grid=(t // tm,),
        in_specs=in_specs,
        out_specs=tok(d),
        out_shape=jax.ShapeDtypeStruct((t, d), F32),
        compiler_params=_cparams(48, 1),
        name="proj_ln_" + mode,
    )(*args)


def _ffn_body(x_ref, wg_ref, wu_ref, wd_ref, g_ref, b_ref, o_ref, xb, acc):
    f = pl.program_id(1)

    @pl.when(f == 0)
    def _():
        xb[...] = x_ref[...].astype(BF16)
        acc[...] = jnp.zeros_like(acc)

    g = _dot(xb[...], wg_ref[...])
    u = _dot(xb[...], wu_ref[...])
    acc[...] += _dot((_silu(g) * u).astype(BF16), wd_ref[...])

    @pl.when(f == pl.num_programs(1) - 1)
    def _():
        o_ref[...] = _layer_norm(ALPHA * x_ref[...] + acc[...], g_ref[...], b_ref[...])


def _ffn_ln(x, w_gate, w_up, w_down, ln_g, ln_b):
    t, d = x.shape
    fdim = w_gate.shape[1]
    tm, tf = _tile(t, 512), _tile(fdim, 1024)
    vec = pl.BlockSpec((1, d), lambda i, f: (0, 0))
    return pl.pallas_call(
        _ffn_body,
        grid=(t // tm, fdim // tf),
        in_specs=[pl.BlockSpec((tm, d), lambda i, f: (i, 0)),
                  pl.BlockSpec((d, tf), lambda i, f: (0, f)),
                  pl.BlockSpec((d, tf), lambda i, f: (0, f)),
                  pl.BlockSpec((tf, d), lambda i, f: (f, 0)),
                  vec, vec],
        out_specs=pl.BlockSpec((tm, d), lambda i, f: (i, 0)),
        out_shape=jax.ShapeDtypeStruct((t, d), F32),
        scratch_shapes=[pltpu.VMEM((tm, d), BF16), pltpu.VMEM((tm, d), F32)],
        compiler_params=_cparams(56, 2),
        name="ffn_ln",
    )(x, w_gate, w_up, w_down, ln_g.reshape(1, d), ln_b.reshape(1, d))


DIFF_KEY_RADIX = 256
DIFF_SC_PER_HEAD = 7


def _split3_f32(x):
    hi = x.astype(BF16).astype(F32)
    mid = (x - hi).astype(BF16).astype(F32)
    lo = (x - hi - mid).astype(BF16).astype(F32)
    return hi, mid, lo


def _lane_tiles(x, op):
    return functools.reduce(op, [x[:, t * LANES:(t + 1) * LANES] for t in range(x.shape[1] // LANES)])


def _diff_attn_body(sc_ref, q_ref, k_ref, v_ref, kx_ref, sg_ref, o_ref, qa, sfull, oacc, m128, l128,
                    *, seq, tq, nqt, out_scale):
    hd = DIFF_HEAD_DIM
    tk = tq
    nk = seq // tk
    h = pl.program_id(1)
    lam = sc_ref[0]
    base = 1 + DIFF_SC_PER_HEAD * h
    slope2 = sc_ref[base + 6]

    irel = lax.broadcasted_iota(I32, (tq, 1), 0).astype(F32)
    lane = lax.broadcasted_iota(I32, (tq, hd), 1)
    ext = jnp.zeros((tq, hd), F32)
    for j, v in enumerate((*_split3_f32(-slope2 * irel), *[sc_ref[base + t] for t in range(6)])):
        ext = jnp.where(lane == j, v, ext)
    for qt in range(nqt):
        for c in range(2):
            qc = q_ref[qt * tq:(qt + 1) * tq, c * hd:(c + 1) * hd]
            qa[qt, c, 0] = jnp.concatenate([qc, ext.astype(BF16)], axis=1)
            qa[qt, c, 1] = jnp.concatenate([qc, (-ext).astype(BF16)], axis=1)
    d = (lax.broadcasted_iota(I32, (tq, tk), 0) - lax.broadcasted_iota(I32, (tq, tk), 1)).astype(F32)
    fix = (2.0 * slope2) * jnp.minimum(d, 0.0)
    m128[...] = jnp.full_like(m128, -jnp.inf)
    l128[...] = jnp.zeros_like(l128)
    oacc[...] = jnp.zeros_like(oacc)

    def tile(qt, r):
        iq = pl.program_id(2) * nqt + qt
        kt = lax.rem(iq + r, nk)
        j0 = pl.multiple_of(kt * tk, tk)
        if r == 0:
            return j0, 0, 0.0
        c0 = ((iq - kt) * tq).astype(F32)
        right = kt > iq
        return j0, right.astype(I32), jnp.where(right, slope2 * c0, -slope2 * c0)

    def scores(qt, r):
        j0, sgn, kappa = tile(qt, r)
        kk = k_ref[pl.ds(j0, tk), :]
        for c in range(2):
            ka = jnp.concatenate([kk[:, c * hd:(c + 1) * hd], kx_ref[...]], axis=1)
            s = _dot_nt(qa[qt, c, sgn], ka)
            if r == 0:
                s = s + fix
            sfull[qt, c, :, r * tk:(r + 1) * tk] = s
            m128[qt, c] = jnp.maximum(m128[qt, c], _lane_tiles(s, jnp.maximum) + kappa)

    def values(qt, r, mrow):
        j0, _, kappa = tile(qt, r)
        vv = v_ref[pl.ds(j0, tk), :]
        for c in range(2):
            p = jnp.exp2(sfull[qt, c, :, r * tk:(r + 1) * tk] - (mrow[c] - kappa))
            l128[qt, c] += _lane_tiles(p, jnp.add)
            oacc[qt, c] += _dot(p.astype(BF16), vv)

    def row_max(qt):
        return [jnp.max(m128[qt, c], axis=-1, keepdims=True) for c in range(2)]

    def finish(qt):
        lsum = [jnp.sum(l128[qt, c], axis=-1, keepdims=True) for c in range(2)]
        o = oacc[qt, 0] / lsum[0] - lam * (oacc[qt, 1] / lsum[1])
        o = o * lax.rsqrt(jnp.mean(o * o, axis=-1, keepdims=True) + SUBLN_EPS) * (sg_ref[...] * out_scale)
        o_ref[qt * tq:(qt + 1) * tq, :] = o.astype(o_ref.dtype)

    for r in range(nk):
        scores(0, r)
    for qt in range(nqt):
        mrow = row_max(qt)
        for r in range(nk):
            if qt + 1 < nqt:
                scores(qt + 1, r)
            values(qt, r, mrow)
        finish(qt)


def _diff_attn(qkv, scalars, subln_g, batch, seq, n_heads, out_scale):
    t = qkv.shape[0]
    hd = DIFF_HEAD_DIM
    vd = 2 * hd
    tq = _tile(seq, 512)
    nqt = 2 if (seq // tq) % 2 == 0 else 1
    assert tq <= DIFF_KEY_RADIX * DIFF_KEY_RADIX
    nq = seq // (tq * nqt)
    lane = jnp.arange(hd)[None, :]
    jrel = jnp.arange(tq)[:, None]
    jl = (jrel % DIFF_KEY_RADIX).astype(F32)
    jh = (jrel // DIFF_KEY_RADIX).astype(F32)
    kx = jnp.where(lane < 3, 1.0, jnp.where(lane < 6, jl, jnp.where(lane < 9, jh, 0.0))).astype(BF16)
    return pl.pallas_call(
        functools.partial(_diff_attn_body, seq=seq, tq=tq, nqt=nqt, out_scale=out_scale),
        grid=(batch, n_heads, nq),
        in_specs=[pl.BlockSpec(memory_space=pltpu.SMEM),
                  pl.BlockSpec((nqt * tq, vd), lambda b, h, i: (b * nq + i, h)),
                  pl.BlockSpec((seq, vd), lambda b, h, i: (b, n_heads + h)),
                  pl.BlockSpec((seq, vd), lambda b, h, i: (b, 2 * n_heads + h)),
                  pl.BlockSpec((tq, hd), lambda b, h, i: (0, 0)),
                  pl.BlockSpec((1, vd), lambda b, h, i: (0, 0))],
        out_specs=pl.BlockSpec((nqt * tq, vd), lambda b, h, i: (b * nq + i, h)),
        out_shape=jax.ShapeDtypeStruct((t, n_heads * vd), BF16),
        scratch_shapes=[pltpu.VMEM((nqt, 2, 2, tq, vd), BF16), pltpu.VMEM((nqt, 2, tq, seq), F32),
                        pltpu.VMEM((nqt, 2, tq, vd), F32), pltpu.VMEM((nqt, 2, tq, LANES), F32),
                        pltpu.VMEM((nqt, 2, tq, LANES), F32)],
        compiler_params=_cparams(58, 3),
        name="diff_attn",
    )(scalars, qkv, qkv, qkv, kx, subln_g.reshape(1, vd))


def _win_attn_body(sc_ref, q_ref, kp_ref, kc_ref, kn_ref, vp_ref, vc_ref, vn_ref, o_ref,
                   *, nb, n_heads):
    d, blk = SWA_HEAD_DIM, SWA_BLOCK
    n = pl.program_id(1)
    grp = n_heads // SWA_KV_HEADS
    rows = grp * blk
    r = lax.broadcasted_iota(I32, (rows, 3 * blk), 0) % blk
    c = lax.broadcasted_iota(I32, (rows, 3 * blk), 1)
    rel = jnp.abs(r - c + blk)
    valid = (rel <= blk) & ((c >= blk) | (n > 0)) & ((c < 2 * blk) | (n < nb - 1))
    relf = rel.astype(F32)
    mask = jnp.where(valid, 0.0, -jnp.inf)
    head_of_row = lax.broadcasted_iota(I32, (rows, 1), 0) // blk
    for kv in range(SWA_KV_HEADS):
        cs = slice(kv * d, (kv + 1) * d)
        kw = jnp.concatenate([kp_ref[:, cs], kc_ref[:, cs], kn_ref[:, cs]], axis=0)
        vw = jnp.concatenate([vp_ref[:, cs], vc_ref[:, cs], vn_ref[:, cs]], axis=0)
        qg = jnp.concatenate([q_ref[:, (kv * grp + g) * d:(kv * grp + g + 1) * d] for g in range(grp)], axis=0)
        slope = jnp.zeros((rows, 1), F32)
        sink = jnp.zeros((rows, 1), F32)
        for g in range(grp):
            slope = jnp.where(head_of_row == g, sc_ref[kv * grp + g], slope)
            sink = jnp.where(head_of_row == g, sc_ref[n_heads + kv * grp + g], sink)
        s = _dot_nt(qg, kw) - slope * relf + mask
        m = jnp.maximum(jnp.max(s, axis=-1, keepdims=True), sink)
        p = jnp.exp2(s - m)
        denom = jnp.sum(p, axis=-1, keepdims=True) + jnp.exp2(sink - m)
        o = _dot(p.astype(BF16), vw) / denom
        for g in range(grp):
            hcol = (kv * grp + g) * d
            o_ref[:, hcol:hcol + d] = o[g * blk:(g + 1) * blk, :].astype(o_ref.dtype)


def _win_attn(qkv, scalars, batch, seq, n_heads):
    t = qkv.shape[0]
    d, blk = SWA_HEAD_DIM, SWA_BLOCK
    nb = seq // blk
    qw, kvw = n_heads * d, SWA_KV_HEADS * d
    assert qw % kvw == 0
    kcol, vcol = qw // kvw, qw // kvw + 1
    prev = lambda b, n: b * nb + jnp.maximum(n - 1, 0)
    nxt = lambda b, n: b * nb + jnp.minimum(n + 1, nb - 1)
    cur = lambda b, n: b * nb + n
    kvspec = lambda f, col: pl.BlockSpec((blk, kvw), lambda b, n: (f(b, n), col))
    return pl.pallas_call(
        functools.partial(_win_attn_body, nb=nb, n_heads=n_heads),
        grid=(batch, nb),
        in_specs=[pl.BlockSpec(memory_space=pltpu.SMEM),
                  pl.BlockSpec((blk, qw), lambda b, n: (cur(b, n), 0)),
                  kvspec(prev, kcol), kvspec(cur, kcol), kvspec(nxt, kcol),
                  kvspec(prev, vcol), kvspec(cur, vcol), kvspec(nxt, vcol)],
        out_specs=pl.BlockSpec((blk, qw), lambda b, n: (cur(b, n), 0)),
        out_shape=jax.ShapeDtypeStruct((t, qw), BF16),
        compiler_params=_cparams(32, 2),
        name="win_attn",
    )(scalars, qkv, qkv, qkv, qkv, qkv, qkv, qkv)


def _lru_body(r_hbm, cw_ref, cb_ref, gw_ref, gb_ref, c_ref, h_hbm,
              rbuf, hout, abuf, bbuf, hcar, isem, osem, *, seq, ts, batch, reverse, left):
    i = pl.program_id(0)
    nch = seq // ts
    taps = cw_ref.shape[0]
    right = taps - 1 - left
    hal = LRU_HALO
    width = r_hbm.shape[2]
    nblk, bw = gw_ref.shape[1], gw_ref.shape[2]
    chunk = lambda s: (nch - 1 - s) if reverse else s
    slot = i % 2

    def in_copies(ci, sl, op):
        t0 = ci * ts

        def each(src_start, dst_start, rows):
            src_start = pl.multiple_of(src_start, LRU_HALO)
            for b in range(batch):
                cp = pltpu.make_async_copy(r_hbm.at[b, pl.ds(src_start, rows), :],
                                           rbuf.at[sl, pl.ds(dst_start, rows), b, :], isem.at[sl])
                getattr(cp, op)()

        each(t0, hal, ts)
        pl.when(ci > 0)(lambda: each(jnp.maximum(t0 - hal, 0), 0, hal))
        pl.when(ci < nch - 1)(lambda: each(jnp.minimum(t0 + ts, seq - hal), hal + ts, hal))

    def out_copies(ci, sl, op):
        for b in range(batch):
            cp = pltpu.make_async_copy(hout.at[sl, pl.ds(0, ts), b, :],
                                       h_hbm.at[b, pl.ds(ci * ts, ts), :], osem.at[sl])
            getattr(cp, op)()

    @pl.when(i == 0)
    def _():
        hcar[...] = jnp.zeros_like(hcar)
        in_copies(chunk(0), 0, "start")

    @pl.when(i + 1 < nch)
    def _():
        in_copies(chunk(i + 1), 1 - slot, "start")

    ci = chunk(i)
    in_copies(ci, slot, "wait")

    @pl.when(ci == 0)
    def _():
        rbuf[slot, hal - left:hal] = jnp.zeros((left, batch, width), F32)

    @pl.when(ci == nch - 1)
    def _():
        rbuf[slot, hal + ts:hal + ts + right] = jnp.zeros((right, batch, width), F32)

    rf = jnp.broadcast_to(cb_ref[...], (ts, batch, width))
    for k in range(taps):
        rf = rf + cw_ref[k:k + 1, :] * rbuf[slot, pl.ds(hal - left + k, ts)]
    rf2 = rf.reshape(ts * batch, width)
    for nbk in range(nblk):
        cs = slice(nbk * bw, (nbk + 1) * bw)
        xb = rf2[:, cs]
        xbb = xb.astype(BF16)
        rg = _sigmoid(_dot(xbb, gw_ref[0, nbk]) + gb_ref[0:1, cs])
        ig = _sigmoid(_dot(xbb, gw_ref[1, nbk]) + gb_ref[1:2, cs])
        a = jnp.exp(c_ref[:, cs] * rg)
        bb = jnp.sqrt(1.0 - a * a) * (ig * xb)
        abuf[:, :, cs] = a.reshape(ts, batch, bw)
        bbuf[:, :, cs] = bb.reshape(ts, batch, bw)

    @pl.when(i >= 2)
    def _():
        out_copies(ci, slot, "wait")

    def step(s, h):
        t = (ts - 1 - s) if reverse else s
        h = abuf[t] * h + bbuf[t]
        hout[slot, t] = h
        return h

    hcar[...] = lax.fori_loop(0, ts, step, hcar[...])
    out_copies(ci, slot, "start")

    @pl.when(i == nch - 1)
    def _():
        out_copies(ci, slot, "wait")
        if nch >= 2:
            out_copies(ci, 1 - slot, "wait")


def _lru_dir(r3, conv_w, conv_b, gate_w, gate_b, cvec, *, reverse, left):
    batch, seq, width = r3.shape
    ts = _tile(seq, 512 // batch)
    taps = conv_w.shape[0]
    assert batch % 8 == 0 and ts % LRU_HALO == 0 and max(left, taps - 1 - left) <= LRU_HALO
    full = lambda a: pl.BlockSpec(a.shape, lambda i: (0,) * a.ndim)
    cb = conv_b.reshape(1, width)
    return pl.pallas_call(
        functools.partial(_lru_body, seq=seq, ts=ts, batch=batch, reverse=reverse, left=left),
        grid=(seq // ts,),
        in_specs=[pl.BlockSpec(memory_space=pl.ANY), full(conv_w), full(cb), full(gate_w),
                  full(gate_b), full(cvec)],
        out_specs=pl.BlockSpec(memory_space=pl.ANY),
        out_shape=jax.ShapeDtypeStruct((batch, seq, width), F32),
        scratch_shapes=[pltpu.VMEM((2, ts + 2 * LRU_HALO, batch, width), F32),
                        pltpu.VMEM((2, ts, batch, width), F32),
                        pltpu.VMEM((ts, batch, width), F32),
                        pltpu.VMEM((ts, batch, width), F32),
                        pltpu.VMEM((batch, width), F32),
                        pltpu.SemaphoreType.DMA((2,)),
                        pltpu.SemaphoreType.DMA((2,))],
        compiler_params=_cparams(48, 1),
        name="lru_rev" if reverse else "lru_fwd",
    )(r3, conv_w, cb, gate_w, gate_b, cvec)


def _split_bf16(x):
    hi = x.astype(BF16)
    lo = (x - hi.astype(F32)).astype(BF16)
    return hi, lo


def _router_body(x_ref, wr_ref, o_ref, cnt_ref, cnt, lower):
    i = pl.program_id(0)
    tm = x_ref.shape[0]
    ne = wr_ref.shape[1]

    @pl.when(i == 0)
    def _():
        cnt[...] = jnp.zeros_like(cnt)
        lower[...] = jnp.where(lax.broadcasted_iota(I32, (tm, tm), 1) < lax.broadcasted_iota(I32, (tm, tm), 0),
                               1.0, 0.0).astype(BF16)

    xh, xl = _split_bf16(x_ref[...])
    wh, wl = _split_bf16(wr_ref[...])
    logits = _dot(xh, wh) + _dot(xl, wh) + _dot(xh, wl)
    lane = lax.broadcasted_iota(I32, (tm, ne), 1).astype(F32)
    v1 = jnp.max(logits, axis=-1, keepdims=True)
    i1 = jnp.min(jnp.where(logits == v1, lane, float(ne)), axis=-1, keepdims=True)
    rest = jnp.where(lane == i1, -jnp.inf, logits)
    v2 = jnp.max(rest, axis=-1, keepdims=True)
    i2 = jnp.min(jnp.where(rest == v2, lane, float(ne)), axis=-1, keepdims=True)
    e2 = jnp.exp(v2 - v1)
    w1 = 1.0 / (1.0 + e2)
    w2 = e2 / (1.0 + e2)
    oh1 = lane == i1
    oh2 = lane == i2
    oh = jnp.where(oh1 | oh2, 1.0, 0.0)
    before = _dot(lower[...], oh.astype(BF16)) + cnt[...]
    r1 = jnp.sum(jnp.where(oh1, before, 0.0), axis=-1, keepdims=True)
    r2 = jnp.sum(jnp.where(oh2, before, 0.0), axis=-1, keepdims=True)
    cnt[...] += jnp.sum(oh, axis=0, keepdims=True)
    cnt_ref[...] = cnt[...]
    col = lax.broadcasted_iota(I32, (tm, 8), 1)
    out = jnp.zeros((tm, 8), F32)
    for j, v in enumerate((i1, i2, w1, w2, r1, r2)):
        out = jnp.where(col == j, v, out)
    o_ref[...] = out


def _router(x, w_router):
    t, d = x.shape
    ne = w_router.shape[1]
    tm = _tile(t, 512)
    return pl.pallas_call(
        _router_body,
        grid=(t // tm,),
        in_specs=[pl.BlockSpec((tm, d), lambda i: (i, 0)),
                  pl.BlockSpec((d, ne), lambda i: (0, 0))],
        out_specs=[pl.BlockSpec((tm, 8), lambda i: (i, 0)),
                   pl.BlockSpec((1, ne), lambda i: (0, 0))],
        out_shape=[jax.ShapeDtypeStruct((t, 8), F32), jax.ShapeDtypeStruct((1, ne), F32)],
        scratch_shapes=[pltpu.VMEM((1, ne), F32), pltpu.VMEM((tm, tm), BF16)],
        compiler_params=_cparams(32, 1),
        name="router",
    )(x, w_router)


ROW_DMA_GROUP = 8
N_DMA_PRIORITIES = 2


def _start_row_gather(idx_ref, n_rows, src_hbm, dst, sem):
    def group(g, carry):
        for j in range(ROW_DMA_GROUP):
            r = g * ROW_DMA_GROUP + j
            row = idx_ref[0, 0, r]
            pltpu.make_async_copy(src_hbm.at[pl.ds(row, 1), :], dst.at[pl.ds(r, 1), :], sem).start(
                priority=j % N_DMA_PRIORITIES)
        return carry
    lax.fori_loop(0, n_rows // ROW_DMA_GROUP, group, 0)


def _wait_rows(n_rows, hbm, vmem, sem):
    pltpu.make_async_copy(hbm.at[pl.ds(0, n_rows), :], vmem, sem).wait()


def _dispatch_body(off_ref, pad_ref, nv_ref, pos_ref, x_ref, xs_hbm, zbuf, sem):
    i = pl.program_id(0)
    tm = x_ref.shape[0]
    tz = zbuf.shape[0]

    @pl.when(i == 0)
    def _():
        zbuf[...] = jnp.zeros_like(zbuf)

        def zero_tile(row0):
            row0 = pl.multiple_of(row0, tz)
            cp = pltpu.make_async_copy(zbuf, xs_hbm.at[pl.ds(row0, tz), :], sem.at[1])
            cp.start()
            cp.wait()

        for e in range(off_ref.shape[0]):
            pl.when(pad_ref[e] > 0)(functools.partial(zero_tile, off_ref[e] + pad_ref[e] - tz))

        def tail(j, carry):
            zero_tile(j * tz)
            return carry
        lax.fori_loop(nv_ref[0], xs_hbm.shape[0] // tz, tail, 0)

    def group(g, carry):
        for j in range(ROW_DMA_GROUP):
            r = g * ROW_DMA_GROUP + j
            for slot in range(TOP_K):
                dst = pos_ref[0, 0, slot * tm + r]
                pltpu.make_async_copy(x_ref.at[pl.ds(r, 1), :], xs_hbm.at[pl.ds(dst, 1), :], sem.at[0]).start(
                    priority=slot % N_DMA_PRIORITIES)
        return carry

    lax.fori_loop(0, tm // ROW_DMA_GROUP, group, 0)
    for _ in range(TOP_K):
        _wait_rows(tm, xs_hbm, x_ref, sem.at[0])


def _dispatch(x, pos_tiles, offsets, padded, n_valid, n_rows, tm, tm_e):
    t, d = x.shape
    grid_spec = pltpu.PrefetchScalarGridSpec(
        num_scalar_prefetch=3,
        grid=(t // tm,),
        in_specs=[pl.BlockSpec((1, 1, TOP_K * tm), lambda i, o, p, nv: (i, 0, 0), memory_space=pltpu.SMEM),
                  pl.BlockSpec((tm, d), lambda i, o, p, nv: (i, 0))],
        out_specs=pl.BlockSpec(memory_space=pl.ANY),
        scratch_shapes=[pltpu.VMEM((tm_e, d), F32), pltpu.SemaphoreType.DMA((2,))],
    )
    return pl.pallas_call(
        _dispatch_body,
        grid_spec=grid_spec,
        out_shape=jax.ShapeDtypeStruct((n_rows, d), F32),
        compiler_params=_cparams(32, 1),
        name="moe_dispatch",
    )(offsets, padded, n_valid, pos_tiles, x)


def _expert_body(te_ref, nv_ref, x_ref, wg_ref, wu_ref, wd_ref, o_ref):
    j = pl.program_id(0)

    @pl.when(j < nv_ref[0])
    def _():
        xb = x_ref[...].astype(BF16)
        g = _dot(xb, wg_ref[0])
        u = _dot(xb, wu_ref[0])
        o_ref[...] = _dot((_silu(g) * u).astype(BF16), wd_ref[0])

    @pl.when(j >= nv_ref[0])
    def _():
        o_ref[...] = jnp.zeros_like(o_ref)


def _experts(xs, tile_expert, n_valid, w_gate, w_up, w_down, tm):
    n_rows, d = xs.shape
    fdim = w_gate.shape[2]
    nt = n_rows // tm
    wspec = lambda shape: pl.BlockSpec((1,) + shape, lambda j, te, nv: (te[j], 0, 0))
    grid_spec = pltpu.PrefetchScalarGridSpec(
        num_scalar_prefetch=2,
        grid=(nt,),
        in_specs=[pl.BlockSpec((tm, d), lambda j, te, nv: (jnp.minimum(j, nv[0] - 1), 0)),
                  wspec((d, fdim)), wspec((d, fdim)), wspec((fdim, d))],
        out_specs=pl.BlockSpec((tm, d), lambda j, te, nv: (j, 0)),
    )
    return pl.pallas_call(
        _expert_body,
        grid_spec=grid_spec,
        out_shape=jax.ShapeDtypeStruct((n_rows, d), F32),
        compiler_params=_cparams(56, 1),
        name="experts",
    )(tile_expert, n_valid, xs, w_gate, w_up, w_down)


def _combine_body(pos_cur, pos_nxt, x_ref, route_ref, g_ref, b_ref, ys_hbm, o_ref, buf, sem):
    i = pl.program_id(0)
    n = pl.num_programs(0)
    tm = x_ref.shape[0]
    slot = i % 2

    @pl.when(i == 0)
    def _():
        _start_row_gather(pos_cur, TOP_K * tm, ys_hbm, buf.at[0], sem.at[0])

    @pl.when(i + 1 < n)
    def _():
        _start_row_gather(pos_nxt, TOP_K * tm, ys_hbm, buf.at[1 - slot], sem.at[1 - slot])

    _wait_rows(TOP_K * tm, ys_hbm, buf.at[slot], sem.at[slot])
    f = route_ref[:, 2:3] * buf[slot, 0:tm, :] + route_ref[:, 3:4] * buf[slot, tm:2 * tm, :]
    o_ref[...] = _layer_norm(ALPHA * x_ref[...] + f, g_ref[...], b_ref[...])


def _combine_ln(x, ys, route, pos, ln_g, ln_b, tm):
    t, d = x.shape
    n = t // tm
    pos_spec = lambda f: pl.BlockSpec((1, 1, TOP_K * tm), lambda i: (f(i), 0, 0), memory_space=pltpu.SMEM)
    vec = pl.BlockSpec((1, d), lambda i: (0, 0))
    return pl.pallas_call(
        _combine_body,
        grid=(n,),
        in_specs=[pos_spec(lambda i: i), pos_spec(lambda i: jnp.minimum(i + 1, n - 1)),
                  pl.BlockSpec((tm, d), lambda i: (i, 0)),
                  pl.BlockSpec((tm, route.shape[1]), lambda i: (i, 0)), vec, vec,
                  pl.BlockSpec(memory_space=pl.ANY)],
        out_specs=pl.BlockSpec((tm, d), lambda i: (i, 0)),
        out_shape=jax.ShapeDtypeStruct((t, d), F32),
        scratch_shapes=[pltpu.VMEM((2, TOP_K * tm, d), F32), pltpu.SemaphoreType.DMA((2,))],
        compiler_params=_cparams(40, 1),
        name="moe_combine_ln",
    )(pos, pos, x, route, ln_g.reshape(1, d), ln_b.reshape(1, d), ys)


def _moe_ln(x, w_router, w_gate, w_up, w_down, ln_g, ln_b):
    t, d = x.shape
    ne = w_router.shape[1]
    tm_e = _tile(t, 512)
    tm_c = _tile(t, 256)
    route, counts = _router(x, w_router)
    expert = route[:, 0:2].astype(I32)
    rank = route[:, 4:6].astype(I32)
    counts = counts[0].astype(I32)
    padded = ((counts + tm_e - 1) // tm_e) * tm_e
    ends = jnp.cumsum(padded)
    offsets = ends - padded
    pos = offsets[expert] + rank
    nt = (TOP_K * t) // tm_e + ne
    tile_expert = jnp.minimum(
        jnp.searchsorted(ends, jnp.arange(nt, dtype=I32) * tm_e, side="right"), ne - 1).astype(I32)
    n_valid = (ends[-1] // tm_e).astype(I32).reshape(1)
    tiles = lambda tm: pos.reshape(t // tm, tm, TOP_K).transpose(0, 2, 1).reshape(t // tm, 1, TOP_K * tm)
    xs = _dispatch(x, tiles(tm_c), offsets, padded, n_valid, nt * tm_e, tm_c, tm_e)
    ys = _experts(xs, tile_expert, n_valid, w_gate, w_up, w_down, tm_e)
    return _combine_ln(x, ys, route, tiles(tm_c), ln_g, ln_b, tm_c)


def _alibi_slopes(n_heads):
    return 2.0 ** (-8.0 * jnp.arange(1, n_heads + 1, dtype=F32) / n_heads)


def kernel(x_prompt, x_sample, l0_conv_w_in, l0_conv_b_in, l0_conv_w_dw, l0_conv_b_dw, l0_conv_norm_g, l0_conv_norm_b, l0_conv_w_out, l0_conv_b_out, l0_ln1_g, l0_ln1_b, l0_ffn_w_gate, l0_ffn_w_up, l0_ffn_w_down, l0_ln2_g, l0_ln2_b, l1_attn_w_qkv, l1_attn_lambda, l1_attn_subln_g, l1_attn_w_out, l1_ln1_g, l1_ln1_b, l1_moe_w_router, l1_moe_w_gate, l1_moe_w_up, l1_moe_w_down, l1_ln2_g, l1_ln2_b, l2_rec_w_in, l2_rec_b_in, l2_rec_conv_w, l2_rec_conv_b, l2_rec_gate_w, l2_rec_gate_b, l2_rec_lambda, l2_rec_w_out, l2_rec_b_out, l2_ln1_g, l2_ln1_b, l2_ffn_w_gate, l2_ffn_w_up, l2_ffn_w_down, l2_ln2_g, l2_ln2_b, l3_attn_w_qkv, l3_attn_sink, l3_attn_w_out, l3_ln1_g, l3_ln1_b, l3_moe_w_router, l3_moe_w_gate, l3_moe_w_up, l3_moe_w_down, l3_ln2_g, l3_ln2_b):
    d = x_prompt.shape[-1]
    bf = lambda w: w.astype(BF16)

    diff_w = l1_attn_w_qkv.shape[1] // 3
    n_diff = diff_w // (2 * DIFF_HEAD_DIM)
    col_scale = jnp.concatenate([jnp.full((diff_w,), LOG2E * DIFF_HEAD_DIM ** -0.5, F32), jnp.ones((2 * diff_w,), F32)])
    w_qkv1 = bf(l1_attn_w_qkv * col_scale)
    lam = l1_attn_lambda.astype(F32)
    lambda_init = 0.8 - 0.6 * math.exp(-0.3 * 1)
    lam_full = jnp.exp(jnp.sum(lam[0] * lam[1])) - jnp.exp(jnp.sum(lam[2] * lam[3])) + lambda_init
    slope2 = LOG2E * _alibi_slopes(n_diff)
    diff_scalars = jnp.concatenate(
        [lam_full.reshape(1),
         jnp.stack([*_split3_f32(slope2), *_split3_f32(DIFF_KEY_RADIX * slope2), slope2], axis=1).reshape(-1)])

    lru_w = l2_rec_w_out.shape[0]
    lru_left = l2_rec_conv_w.shape[0] // 2
    lru_c = -LRU_C * jax.nn.softplus(-l2_rec_lambda.astype(F32))
    gate_w = bf(l2_rec_gate_w)

    n_swa = d // SWA_HEAD_DIM
    swa_q = n_swa * SWA_HEAD_DIM
    col_scale3 = jnp.concatenate([jnp.full((swa_q,), LOG2E * SWA_HEAD_DIM ** -0.5, F32),
                                  jnp.ones((l3_attn_w_qkv.shape[1] - swa_q,), F32)])
    w_qkv3 = bf(l3_attn_w_qkv * col_scale3)
    swa_scalars = LOG2E * jnp.concatenate([_alibi_slopes(n_swa), l3_attn_sink.astype(F32)])

    w_in0, w_out0 = bf(l0_conv_w_in), bf(l0_conv_w_out)
    ffn0 = (bf(l0_ffn_w_gate), bf(l0_ffn_w_up), bf(l0_ffn_w_down))
    w_out1 = bf(l1_attn_w_out)
    moe1 = (bf(l1_moe_w_gate), bf(l1_moe_w_up), bf(l1_moe_w_down))
    w_in2, w_out2 = bf(l2_rec_w_in), bf(l2_rec_w_out)
    ffn2 = (bf(l2_ffn_w_gate), bf(l2_ffn_w_up), bf(l2_ffn_w_down))
    w_out3 = bf(l3_attn_w_out)
    moe3 = (bf(l3_moe_w_gate), bf(l3_moe_w_up), bf(l3_moe_w_down))
    b_in0 = l0_conv_b_in.reshape(1, -1)
    b_in2 = l2_rec_b_in.reshape(1, -1)

    def trunk(x3):
        batch, seq, _ = x3.shape
        x = x3.reshape(batch * seq, d)
        glu = _mm_pair(x, w_in0, b_in0, mode="glu")
        x = _conv_out_ln(glu, x, l0_conv_w_dw, l0_conv_b_dw, l0_conv_norm_g, l0_conv_norm_b,
                         w_out0, l0_conv_b_out, l0_ln1_g, l0_ln1_b, seq)
        x = _ffn_ln(x, *ffn0, l0_ln2_g, l0_ln2_b)
        qkv = _mm(x, w_qkv1, BF16)
        o = _diff_attn(qkv, diff_scalars, l1_attn_subln_g, batch, seq, n_diff, 1.0 - lambda_init)
        x = _proj_ln([o], x, w_out1, None, l1_ln1_g, l1_ln1_b)
        x = _moe_ln(x, l1_moe_w_router, *moe1, l1_ln2_g, l1_ln2_b)
        y, r = _mm_pair(x, w_in2, b_in2, mode="gelu")
        r = r.reshape(batch, seq, lru_w)
        hs = [_lru_dir(r, l2_rec_conv_w, l2_rec_conv_b, gate_w[dr], l2_rec_gate_b[dr], lru_c[dr:dr + 1],
                       reverse=bool(dr), left=lru_left).reshape(batch * seq, lru_w) for dr in range(2)]
        x = _proj_ln([hs[0], hs[1], y], x, w_out2, l2_rec_b_out, l2_ln1_g, l2_ln1_b, mode="lru")
        x = _ffn_ln(x, *ffn2, l2_ln2_g, l2_ln2_b)
        qkv = _mm(x, w_qkv3, BF16)
        o = _win_attn(qkv, swa_scalars, batch, seq, n_swa)
        x = _proj_ln([o], x, w_out3, None, l3_ln1_g, l3_ln1_b)
        x = _moe_ln(x, l3_moe_w_router, *moe3, l3_ln2_g, l3_ln2_b)
        return x.reshape(batch, seq, d)

    return (trunk(x_prompt), trunk(x_sample))
```

```python
import functools
import math

import jax
import jax.numpy as jnp
from jax import lax
from jax.experimental import pallas as pl
from jax.experimental.pallas import tpu as pltpu

F32 = jnp.float32
BF16 = jnp.bfloat16
I32 = jnp.int32

N_LAYERS = 4
LN_EPS = 1e-5
SUBLN_EPS = 1e-5
ALPHA = (2 * N_LAYERS) ** 0.25
LRU_C = 8.0
DIFF_HEAD_DIM = 128
SWA_HEAD_DIM = 128
SWA_KV_HEADS = 4
SWA_BLOCK = 128
TOP_K = 2
LOG2E = math.log2(math.e)
LANES = 128
HALO_ROWS = 16
LRU_HALO = 8
MIB = 1024 * 1024


def _cparams(vmem_mib, n_axes):
    return pltpu.CompilerParams(
        dimension_semantics=("arbitrary",) * n_axes, vmem_limit_bytes=vmem_mib * MIB)


def _tile(n, pref):
    t = min(n, pref)
    while n % t:
        t //= 2
    return t


def _sigmoid(x):
    return 0.5 * jnp.tanh(0.5 * x) + 0.5


def _silu(x):
    return x * _sigmoid(x)


def _gelu_tanh(x):
    return 0.5 * x * (1.0 + jnp.tanh(math.sqrt(2.0 / math.pi) * (x + 0.044715 * (x * x * x))))


def _dot(a, b):
    return jnp.dot(a, b, preferred_element_type=F32)


def _dot_nt(a, b):
    return lax.dot_general(a, b, (((1,), (1,)), ((), ())), preferred_element_type=F32)


def _layer_norm(y, g, b):
    mu = jnp.mean(y, axis=-1, keepdims=True)
    yc = y - mu
    var = jnp.mean(yc * yc, axis=-1, keepdims=True)
    return yc * lax.rsqrt(var + LN_EPS) * g + b


def _mm_body(x_ref, w_ref, o_ref, xb):
    @pl.when(pl.program_id(1) == 0)
    def _():
        xb[...] = x_ref[...].astype(BF16)

    o_ref[...] = _dot(xb[...], w_ref[...]).astype(o_ref.dtype)


def _mm(x, w, out_dtype):
    m, k = x.shape
    n = w.shape[1]
    tm, tn = _tile(m, 1024), _tile(n, 2048)
    return pl.pallas_call(
        _mm_body,
        grid=(m // tm, n // tn),
        in_specs=[pl.BlockSpec((tm, k), lambda i, j: (i, 0)),
                  pl.BlockSpec((k, tn), lambda i, j: (0, j))],
        out_specs=pl.BlockSpec((tm, tn), lambda i, j: (i, j)),
        out_shape=jax.ShapeDtypeStruct((m, n), out_dtype),
        scratch_shapes=[pltpu.VMEM((tm, k), BF16)],
        compiler_params=_cparams(56, 2),
        name="mm",
    )(x, w)


def _mm_pair_body(x_ref, wa_ref, wg_ref, ba_ref, bg_ref, *rest, mode):
    xb = rest[-1]

    @pl.when(pl.program_id(1) == 0)
    def _():
        xb[...] = x_ref[...].astype(BF16)

    a = _dot(xb[...], wa_ref[...]) + ba_ref[...]
    g = _dot(xb[...], wg_ref[...]) + bg_ref[...]
    if mode == "glu":
        rest[0][...] = a * _sigmoid(g)
    else:
        rest[0][...] = _gelu_tanh(a).astype(rest[0].dtype)
        rest[1][...] = g


def _mm_pair(x, w, bias, *, mode):
    m, k = x.shape
    n = w.shape[1] // 2
    tm, tn = _tile(m, 1024), _tile(n, 1024)
    nj = n // tn
    out_spec = pl.BlockSpec((tm, tn), lambda i, j: (i, j))
    if mode == "glu":
        out_specs, out_shape = out_spec, jax.ShapeDtypeStruct((m, n), F32)
    else:
        out_specs = [out_spec, out_spec]
        out_shape = [jax.ShapeDtypeStruct((m, n), BF16), jax.ShapeDtypeStruct((m, n), F32)]
    return pl.pallas_call(
        functools.partial(_mm_pair_body, mode=mode),
        grid=(m // tm, nj),
        in_specs=[pl.BlockSpec((tm, k), lambda i, j: (i, 0)),
                  pl.BlockSpec((k, tn), lambda i, j: (0, j)),
                  pl.BlockSpec((k, tn), lambda i, j: (0, j + nj)),
                  pl.BlockSpec((1, tn), lambda i, j: (0, j)),
                  pl.BlockSpec((1, tn), lambda i, j: (0, j + nj))],
        out_specs=out_specs,
        out_shape=out_shape,
        scratch_shapes=[pltpu.VMEM((tm, k), BF16)],
        compiler_params=_cparams(56, 2),
        name="mm_" + mode,
    )(x, w, w, bias, bias)


def _conv_out_body(cur_ref, prv_ref, nxt_ref, res_ref, wdw_ref, bdw_ref, ng_ref, nb_ref,
                   w_ref, bo_ref, g_ref, b_ref, o_ref, ext, cslab, lhs, *, seq, ts, width):
    i = pl.program_id(0)
    t0 = i * ts
    first = (t0 % seq) == 0
    last = ((t0 + ts) % seq) == 0
    nsl = cur_ref.shape[1] // LANES
    half = (width - 1) // 2
    for c in range(nsl):
        cs = slice(c * LANES, (c + 1) * LANES)
        ext[c, 0:HALO_ROWS, :] = jnp.where(first, 0.0, prv_ref[:, cs])
        ext[c, HALO_ROWS:HALO_ROWS + ts, :] = cur_ref[:, cs]
        ext[c, HALO_ROWS + ts:, :] = jnp.where(last, 0.0, nxt_ref[:, cs])

    def slab(c, carry):
        wk = wdw_ref[c]
        acc = jnp.broadcast_to(bdw_ref[c], (ts, LANES))
        for k in range(width):
            acc = acc + wk[k:k + 1, :] * ext[c, pl.ds(HALO_ROWS - half + k, ts), :]
        cslab[c] = acc
        return carry

    lax.fori_loop(0, nsl, slab, 0)

    s1 = cslab[0]
    for c in range(1, nsl):
        s1 = s1 + cslab[c]
    mu = jnp.sum(s1, axis=-1, keepdims=True) * (1.0 / (nsl * LANES))
    s2 = jnp.square(cslab[0] - mu)
    for c in range(1, nsl):
        s2 = s2 + jnp.square(cslab[c] - mu)
    var = jnp.sum(s2, axis=-1, keepdims=True) * (1.0 / (nsl * LANES))
    rstd = lax.rsqrt(var + LN_EPS)
    for c in range(nsl):
        cs = slice(c * LANES, (c + 1) * LANES)
        v = (cslab[c] - mu) * rstd * ng_ref[:, cs] + nb_ref[:, cs]
        lhs[:, cs] = _silu(v).astype(BF16)
    h = _dot(lhs[...], w_ref[...]) + bo_ref[...]
    o_ref[...] = _layer_norm(ALPHA * res_ref[...] + h, g_ref[...], b_ref[...])


def _conv_out_ln(glu, res, w_dw, b_dw, norm_g, norm_b, w_out, b_out, ln_g, ln_b, seq):
    t, d = glu.shape
    width = w_dw.shape[0]
    assert (width - 1) // 2 <= HALO_ROWS and d % LANES == 0
    ts = _tile(seq, 256)
    nsl = d // LANES
    hb = ts // HALO_ROWS
    wdw = w_dw.reshape(width, nsl, LANES).transpose(1, 0, 2)
    bdw = b_dw.reshape(nsl, 1, LANES)
    row = lambda v: v.reshape(1, d)
    vec = pl.BlockSpec((1, d), lambda i: (0, 0))
    return pl.pallas_call(
        functools.partial(_conv_out_body, seq=seq, ts=ts, width=width),
        grid=(t // ts,),
        in_specs=[pl.BlockSpec((ts, d), lambda i: (i, 0)),
                  pl.BlockSpec((HALO_ROWS, d), lambda i: (jnp.maximum(i * hb - 1, 0), 0)),
                  pl.BlockSpec((HALO_ROWS, d), lambda i: (jnp.minimum((i + 1) * hb, t // HALO_ROWS - 1), 0)),
                  pl.BlockSpec((ts, d), lambda i: (i, 0)),
                  pl.BlockSpec((nsl, width, LANES), lambda i: (0, 0, 0)),
                  pl.BlockSpec((nsl, 1, LANES), lambda i: (0, 0, 0)),
                  vec, vec,
                  pl.BlockSpec((d, d), lambda i: (0, 0)),
                  vec, vec, vec],
        out_specs=pl.BlockSpec((ts, d), lambda i: (i, 0)),
        out_shape=jax.ShapeDtypeStruct((t, d), F32),
        scratch_shapes=[pltpu.VMEM((nsl, ts + 2 * HALO_ROWS, LANES), F32),
                        pltpu.VMEM((nsl, ts, LANES), F32),
                        pltpu.VMEM((ts, d), BF16)],
        compiler_params=_cparams(48, 1),
        name="conv_out_ln",
    )(glu, glu, glu, res, wdw, bdw, row(norm_g), row(norm_b), w_out, row(b_out), row(ln_g), row(ln_b))


def _proj_ln_body(*refs, mode, has_bias):
    if mode == "lru":
        hf_ref, hb_ref, y_ref = refs[:3]
        rest = refs[3:]
        lhs = ((hf_ref[...] + hb_ref[...]) * y_ref[...].astype(F32)).astype(BF16)
    else:
        rest = refs[1:]
        lhs = refs[0][...].astype(BF16)
    if has_bias:
        res_ref, w_ref, bo_ref, g_ref, b_ref, o_ref = rest
    else:
        res_ref, w_ref, g_ref, b_ref, o_ref = rest
    h = _dot(lhs, w_ref[...])
    if has_bias:
        h = h + bo_ref[...]
    o_ref[...] = _layer_norm(ALPHA * res_ref[...] + h, g_ref[...], b_ref[...])


def _proj_ln(lhs_args, res, w, b_out, ln_g, ln_b, *, mode="plain"):
    t, d = res.shape
    k = w.shape[0]
    tm = _tile(t, 512)
    tok = lambda n: pl.BlockSpec((tm, n), lambda i: (i, 0))
    vec = pl.BlockSpec((1, d), lambda i: (0, 0))
    row = lambda v: v.reshape(1, d)
    in_specs = [tok(k) for _ in lhs_args] + [tok(d), pl.BlockSpec((k, d), lambda i: (0, 0),
                                                                  pipeline_mode=pl.Buffered(1))]
    args = list(lhs_args) + [res, w]
    if b_out is not None:
        in_specs.append(vec)
        args.append(row(b_out))
    in_specs += [vec, vec]
    args += [row(ln_g), row(ln_b)]
    return pl.pallas_call(
        functools.partial(_proj_ln_body, mode=mode, has_bias=b_out is not None),
        grid=(t // tm,),
        in_specs=in_specs,
        out_specs=tok(d),
        out_shape=jax.ShapeDtypeStruct((t, d), F32),
        compiler_params=_cparams(56, 1),
        name="proj_ln_" + mode,
    )(*args)


def _ffn_body(x_ref, wg_ref, wu_ref, wd_ref, g_ref, b_ref, o_ref, xb, acc):
    f = pl.program_id(1)

    @pl.when(f == 0)
    def _():
        xb[...] = x_ref[...].astype(BF16)
        acc[...] = jnp.zeros_like(acc)

    g = _dot(xb[...], wg_ref[...])
    u = _dot(xb[...], wu_ref[...])
    acc[...] += _dot((_silu(g) * u).astype(BF16), wd_ref[...])

    @pl.when(f == pl.num_programs(1) - 1)
    def _():
        o_ref[...] = _layer_norm(ALPHA * x_ref[...] + acc[...], g_ref[...], b_ref[...])


def _ffn_ln(x, w_gate, w_up, w_down, ln_g, ln_b):
    t, d = x.shape
    fdim = w_gate.shape[1]
    tm, tf = _tile(t, 512), _tile(fdim, 1024)
    vec = pl.BlockSpec((1, d), lambda i, f: (0, 0))
    return pl.pallas_call(
        _ffn_body,
        grid=(t // tm, fdim // tf),
        in_specs=[pl.BlockSpec((tm, d), lambda i, f: (i, 0)),
                  pl.BlockSpec((d, tf), lambda i, f: (0, f)),
                  pl.BlockSpec((d, tf), lambda i, f: (0, f)),
                  pl.BlockSpec((tf, d), lambda i, f: (f, 0)),
                  vec, vec],
        out_specs=pl.BlockSpec((tm, d), lambda i, f: (i, 0)),
        out_shape=jax.ShapeDtypeStruct((t, d), F32),
        scratch_shapes=[pltpu.VMEM((tm, d), BF16), pltpu.VMEM((tm, d), F32)],
        compiler_params=_cparams(56, 2),
        name="ffn_ln",
    )(x, w_gate, w_up, w_down, ln_g.reshape(1, d), ln_b.reshape(1, d))


DIFF_KEY_RADIX = 256
DIFF_SC_PER_HEAD = 7


def _split3_f32(x):
    hi = x.astype(BF16).astype(F32)
    mid = (x - hi).astype(BF16).astype(F32)
    lo = (x - hi - mid).astype(BF16).astype(F32)
    return hi, mid, lo


def _lane_tiles(x, op):
    return functools.reduce(op, [x[:, t * LANES:(t + 1) * LANES] for t in range(x.shape[1] // LANES)])


def _diff_attn_body(sc_ref, q_ref, k_ref, v_ref, kx_ref, sg_ref, o_ref, qa, sfull, oacc, m128, l128,
                    *, seq, tq, nqt, out_scale):
    hd = DIFF_HEAD_DIM
    tk = tq
    nk = seq // tk
    h = pl.program_id(1)
    lam = sc_ref[0]
    base = 1 + DIFF_SC_PER_HEAD * h
    slope2 = sc_ref[base + 6]

    irel = lax.broadcasted_iota(I32, (tq, 1), 0).astype(F32)
    lane = lax.broadcasted_iota(I32, (tq, hd), 1)
    ext = jnp.zeros((tq, hd), F32)
    for j, v in enumerate((*_split3_f32(-slope2 * irel), *[sc_ref[base + t] for t in range(6)])):
        ext = jnp.where(lane == j, v, ext)
    for qt in range(nqt):
        for c in range(2):
            qc = q_ref[qt * tq:(qt + 1) * tq, c * hd:(c + 1) * hd]
            qa[qt, c, 0] = jnp.concatenate([qc, ext.astype(BF16)], axis=1)
            qa[qt, c, 1] = jnp.concatenate([qc, (-ext).astype(BF16)], axis=1)
    d = (lax.broadcasted_iota(I32, (tq, tk), 0) - lax.broadcasted_iota(I32, (tq, tk), 1)).astype(F32)
    fix = (2.0 * slope2) * jnp.minimum(d, 0.0)
    m128[...] = jnp.full_like(m128, -jnp.inf)
    l128[...] = jnp.zeros_like(l128)
    oacc[...] = jnp.zeros_like(oacc)

    def tile(qt, r):
        iq = pl.program_id(2) * nqt + qt
        kt = lax.rem(iq + r, nk)
        j0 = pl.multiple_of(kt * tk, tk)
        if r == 0:
            return j0, 0, 0.0
        c0 = ((iq - kt) * tq).astype(F32)
        right = kt > iq
        return j0, right.astype(I32), jnp.where(right, slope2 * c0, -slope2 * c0)

    def scores(qt, r):
        j0, sgn, kappa = tile(qt, r)
        kk = k_ref[pl.ds(j0, tk), :]
        for c in range(2):
            ka = jnp.concatenate([kk[:, c * hd:(c + 1) * hd], kx_ref[...]], axis=1)
            s = _dot_nt(qa[qt, c, sgn], ka)
            if r == 0:
                s = s + fix
            sfull[qt, c, :, r * tk:(r + 1) * tk] = s
            m128[qt, c] = jnp.maximum(m128[qt, c], _lane_tiles(s, jnp.maximum) + kappa)

    def values(qt, r, mrow):
        j0, _, kappa = tile(qt, r)
        vv = v_ref[pl.ds(j0, tk), :]
        for c in range(2):
            p = jnp.exp2(sfull[qt, c, :, r * tk:(r + 1) * tk] - (mrow[c] - kappa))
            l128[qt, c] += _lane_tiles(p, jnp.add)
            oacc[qt, c] += _dot(p.astype(BF16), vv)

    def row_max(qt):
        return [jnp.max(m128[qt, c], axis=-1, keepdims=True) for c in range(2)]

    def finish(qt):
        lsum = [jnp.sum(l128[qt, c], axis=-1, keepdims=True) for c in range(2)]
        o = oacc[qt, 0] / lsum[0] - lam * (oacc[qt, 1] / lsum[1])
        o = o * lax.rsqrt(jnp.mean(o * o, axis=-1, keepdims=True) + SUBLN_EPS) * (sg_ref[...] * out_scale)
        o_ref[qt * tq:(qt + 1) * tq, :] = o.astype(o_ref.dtype)

    for r in range(nk):
        scores(0, r)
    for qt in range(nqt):
        mrow = row_max(qt)
        for r in range(nk):
            if qt + 1 < nqt:
                scores(qt + 1, r)
            values(qt, r, mrow)
        finish(qt)


def _diff_attn(qkv, scalars, subln_g, batch, seq, n_heads, out_scale):
    t = qkv.shape[0]
    hd = DIFF_HEAD_DIM
    vd = 2 * hd
    tq = _tile(seq, 512)
    nqt = 2 if (seq // tq) % 2 == 0 else 1
    assert tq <= DIFF_KEY_RADIX * DIFF_KEY_RADIX
    nq = seq // (tq * nqt)
    lane = jnp.arange(hd)[None, :]
    jrel = jnp.arange(tq)[:, None]
    jl = (jrel % DIFF_KEY_RADIX).astype(F32)
    jh = (jrel // DIFF_KEY_RADIX).astype(F32)
    kx = jnp.where(lane < 3, 1.0, jnp.where(lane < 6, jl, jnp.where(lane < 9, jh, 0.0))).astype(BF16)
    return pl.pallas_call(
        functools.partial(_diff_attn_body, seq=seq, tq=tq, nqt=nqt, out_scale=out_scale),
        grid=(batch, n_heads, nq),
        in_specs=[pl.BlockSpec(memory_space=pltpu.SMEM),
                  pl.BlockSpec((nqt * tq, vd), lambda b, h, i: (b * nq + i, h)),
                  pl.BlockSpec((seq, vd), lambda b, h, i: (b, n_heads + h)),
                  pl.BlockSpec((seq, vd), lambda b, h, i: (b, 2 * n_heads + h)),
                  pl.BlockSpec((tq, hd), lambda b, h, i: (0, 0)),
                  pl.BlockSpec((1, vd), lambda b, h, i: (0, 0))],
        out_specs=pl.BlockSpec((nqt * tq, vd), lambda b, h, i: (b * nq + i, h)),
        out_shape=jax.ShapeDtypeStruct((t, n_heads * vd), BF16),
        scratch_shapes=[pltpu.VMEM((nqt, 2, 2, tq, vd), BF16), pltpu.VMEM((nqt, 2, tq, seq), F32),
                        pltpu.VMEM((nqt, 2, tq, vd), F32), pltpu.VMEM((nqt, 2, tq, LANES), F32),
                        pltpu.VMEM((nqt, 2, tq, LANES), F32)],
        compiler_params=_cparams(58, 3),
        name="diff_attn",
    )(scalars, qkv, qkv, qkv, kx, subln_g.reshape(1, vd))


def _win_attn_body(sc_ref, bias_ref, q_ref, kp_ref, kc_ref, kn_ref, vp_ref, vc_ref, vn_ref, o_ref,
                   *, n_heads):
    d, blk = SWA_HEAD_DIM, SWA_BLOCK
    grp = n_heads // SWA_KV_HEADS
    rows = grp * blk
    head_of_row = lax.broadcasted_iota(I32, (rows, 1), 0) // blk
    for kv in range(SWA_KV_HEADS):
        cs = slice(kv * d, (kv + 1) * d)
        kw = jnp.concatenate([kp_ref[:, cs], kc_ref[:, cs], kn_ref[:, cs]], axis=0)
        vw = jnp.concatenate([vp_ref[:, cs], vc_ref[:, cs], vn_ref[:, cs]], axis=0)
        qg = jnp.concatenate([q_ref[:, (kv * grp + g) * d:(kv * grp + g + 1) * d] for g in range(grp)], axis=0)
        sink = jnp.zeros((rows, 1), F32)
        for g in range(grp):
            sink = jnp.where(head_of_row == g, sc_ref[kv * grp + g], sink)
        s = _dot_nt(qg, kw) - bias_ref[0, kv]
        m = jnp.maximum(jnp.max(s, axis=-1, keepdims=True), sink)
        p = jnp.exp2(s - m)
        denom = jnp.sum(p, axis=-1, keepdims=True) + jnp.exp2(sink - m)
        o = _dot(p.astype(BF16), vw) / denom
        for g in range(grp):
            hcol = (kv * grp + g) * d
            o_ref[:, hcol:hcol + d] = o[g * blk:(g + 1) * blk, :].astype(o_ref.dtype)


def _win_attn(qkv, slopes, sinks, batch, seq, n_heads):
    t = qkv.shape[0]
    d, blk = SWA_HEAD_DIM, SWA_BLOCK
    nb = seq // blk
    qw, kvw = n_heads * d, SWA_KV_HEADS * d
    assert qw % kvw == 0 and nb >= 2
    grp = n_heads // SWA_KV_HEADS
    rows = grp * blk
    qrow = jnp.arange(rows)[:, None] % blk
    col = jnp.arange(3 * blk)[None, :]
    rel = jnp.abs(qrow - col + blk)
    band = rel <= blk
    valid = jnp.stack([band & (col >= blk), band, band & (col < 2 * blk)])
    slope_rows = jnp.repeat(slopes.reshape(SWA_KV_HEADS, grp), blk, axis=1)
    bias = jnp.where(valid[:, None], slope_rows[None, :, :, None] * rel.astype(F32)[None, None], jnp.inf)
    edge = lambda n: jnp.where(n == 0, 0, jnp.where(n == nb - 1, 2, 1))
    kcol, vcol = qw // kvw, qw // kvw + 1
    prev = lambda b, n: b * nb + jnp.maximum(n - 1, 0)
    nxt = lambda b, n: b * nb + jnp.minimum(n + 1, nb - 1)
    cur = lambda b, n: b * nb + n
    kvspec = lambda f, col: pl.BlockSpec((blk, kvw), lambda b, n: (f(b, n), col))
    return pl.pallas_call(
        functools.partial(_win_attn_body, n_heads=n_heads),
        grid=(batch, nb),
        in_specs=[pl.BlockSpec(memory_space=pltpu.SMEM),
                  pl.BlockSpec((1, SWA_KV_HEADS, rows, 3 * blk), lambda b, n: (edge(n), 0, 0, 0)),
                  pl.BlockSpec((blk, qw), lambda b, n: (cur(b, n), 0)),
                  kvspec(prev, kcol), kvspec(cur, kcol), kvspec(nxt, kcol),
                  kvspec(prev, vcol), kvspec(cur, vcol), kvspec(nxt, vcol)],
        out_specs=pl.BlockSpec((blk, qw), lambda b, n: (cur(b, n), 0)),
        out_shape=jax.ShapeDtypeStruct((t, qw), BF16),
        compiler_params=_cparams(32, 2),
        name="win_attn",
    )(sinks, bias, qkv, qkv, qkv, qkv, qkv, qkv, qkv)


def _lru_body(r_hbm, cw_ref, cb_ref, gw_ref, gb_ref, c_ref, h_hbm,
              rbuf, hout, abuf, bbuf, hcar, isem, osem, *, seq, ts, batch, reverse, left):
    i = pl.program_id(0)
    nch = seq // ts
    taps = cw_ref.shape[0]
    right = taps - 1 - left
    hal = LRU_HALO
    width = r_hbm.shape[2]
    nblk, bw = gw_ref.shape[1], gw_ref.shape[2]
    chunk = lambda s: (nch - 1 - s) if reverse else s
    slot = i % 2

    def in_copies(ci, sl, op):
        t0 = ci * ts

        def each(src_start, dst_start, rows):
            src_start = pl.multiple_of(src_start, LRU_HALO)
            for b in range(batch):
                cp = pltpu.make_async_copy(r_hbm.at[b, pl.ds(src_start, rows), :],
                                           rbuf.at[sl, pl.ds(dst_start, rows), b, :], isem.at[sl])
                getattr(cp, op)()

        each(t0, hal, ts)
        pl.when(ci > 0)(lambda: each(jnp.maximum(t0 - hal, 0), 0, hal))
        pl.when(ci < nch - 1)(lambda: each(jnp.minimum(t0 + ts, seq - hal), hal + ts, hal))

    def out_copies(ci, sl, op):
        for b in range(batch):
            cp = pltpu.make_async_copy(hout.at[sl, pl.ds(0, ts), b, :],
                                       h_hbm.at[b, pl.ds(ci * ts, ts), :], osem.at[sl])
            getattr(cp, op)()

    @pl.when(i == 0)
    def _():
        hcar[...] = jnp.zeros_like(hcar)
        in_copies(chunk(0), 0, "start")

    @pl.when(i + 1 < nch)
    def _():
        in_copies(chunk(i + 1), 1 - slot, "start")

    ci = chunk(i)
    in_copies(ci, slot, "wait")

    @pl.when(ci == 0)
    def _():
        rbuf[slot, hal - left:hal] = jnp.zeros((left, batch, width), F32)

    @pl.when(ci == nch - 1)
    def _():
        rbuf[slot, hal + ts:hal + ts + right] = jnp.zeros((right, batch, width), F32)

    rf = jnp.broadcast_to(cb_ref[...], (ts, batch, width))
    for k in range(taps):
        rf = rf + cw_ref[k:k + 1, :] * rbuf[slot, pl.ds(hal - left + k, ts)]
    rf2 = rf.reshape(ts * batch, width)
    for nbk in range(nblk):
        cs = slice(nbk * bw, (nbk + 1) * bw)
        xb = rf2[:, cs]
        xbb = xb.astype(BF16)
        rg = _sigmoid(_dot(xbb, gw_ref[0, nbk]) + gb_ref[0:1, cs])
        ig = _sigmoid(_dot(xbb, gw_ref[1, nbk]) + gb_ref[1:2, cs])
        a = jnp.exp(c_ref[:, cs] * rg)
        bb = jnp.sqrt(1.0 - a * a) * (ig * xb)
        abuf[:, :, cs] = a.reshape(ts, batch, bw)
        bbuf[:, :, cs] = bb.reshape(ts, batch, bw)

    @pl.when(i >= 2)
    def _():
        out_copies(ci, slot, "wait")

    def step(s, h):
        t = (ts - 1 - s) if reverse else s
        h = abuf[t] * h + bbuf[t]
        hout[slot, t] = h
        return h

    hcar[...] = lax.fori_loop(0, ts, step, hcar[...])
    out_copies(ci, slot, "start")

    @pl.when(i == nch - 1)
    def _():
        out_copies(ci, slot, "wait")
        if nch >= 2:
            out_copies(ci, 1 - slot, "wait")


def _lru_dir(r3, conv_w, conv_b, gate_w, gate_b, cvec, *, reverse, left):
    batch, seq, width = r3.shape
    ts = _tile(seq, 512 // batch)
    taps = conv_w.shape[0]
    assert batch % 8 == 0 and ts % LRU_HALO == 0 and max(left, taps - 1 - left) <= LRU_HALO
    full = lambda a: pl.BlockSpec(a.shape, lambda i: (0,) * a.ndim)
    cb = conv_b.reshape(1, width)
    return pl.pallas_call(
        functools.partial(_lru_body, seq=seq, ts=ts, batch=batch, reverse=reverse, left=left),
        grid=(seq // ts,),
        in_specs=[pl.BlockSpec(memory_space=pl.ANY), full(conv_w), full(cb), full(gate_w),
                  full(gate_b), full(cvec)],
        out_specs=pl.BlockSpec(memory_space=pl.ANY),
        out_shape=jax.ShapeDtypeStruct((batch, seq, width), F32),
        scratch_shapes=[pltpu.VMEM((2, ts + 2 * LRU_HALO, batch, width), F32),
                        pltpu.VMEM((2, ts, batch, width), F32),
                        pltpu.VMEM((ts, batch, width), F32),
                        pltpu.VMEM((ts, batch, width), F32),
                        pltpu.VMEM((batch, width), F32),
                        pltpu.SemaphoreType.DMA((2,)),
                        pltpu.SemaphoreType.DMA((2,))],
        compiler_params=_cparams(48, 1),
        name="lru_rev" if reverse else "lru_fwd",
    )(r3, conv_w, cb, gate_w, gate_b, cvec)


def _split_bf16(x):
    hi = x.astype(BF16)
    lo = (x - hi.astype(F32)).astype(BF16)
    return hi, lo


def _router_body(x_ref, wr_ref, o_ref, cnt_ref, cnt, lower):
    i = pl.program_id(0)
    tm = x_ref.shape[0]
    ne = wr_ref.shape[1]

    @pl.when(i == 0)
    def _():
        cnt[...] = jnp.zeros_like(cnt)
        lower[...] = jnp.where(lax.broadcasted_iota(I32, (tm, tm), 1) < lax.broadcasted_iota(I32, (tm, tm), 0),
                               1.0, 0.0).astype(BF16)

    xh, xl = _split_bf16(x_ref[...])
    wh, wl = _split_bf16(wr_ref[...])
    logits = _dot(xh, wh) + _dot(xl, wh) + _dot(xh, wl)
    lane = lax.broadcasted_iota(I32, (tm, ne), 1).astype(F32)
    v1 = jnp.max(logits, axis=-1, keepdims=True)
    i1 = jnp.min(jnp.where(logits == v1, lane, float(ne)), axis=-1, keepdims=True)
    rest = jnp.where(lane == i1, -jnp.inf, logits)
    v2 = jnp.max(rest, axis=-1, keepdims=True)
    i2 = jnp.min(jnp.where(rest == v2, lane, float(ne)), axis=-1, keepdims=True)
    e2 = jnp.exp(v2 - v1)
    w1 = 1.0 / (1.0 + e2)
    w2 = e2 / (1.0 + e2)
    oh1 = lane == i1
    oh2 = lane == i2
    oh = jnp.where(oh1 | oh2, 1.0, 0.0)
    before = _dot(lower[...], oh.astype(BF16)) + cnt[...]
    r1 = jnp.sum(jnp.where(oh1, before, 0.0), axis=-1, keepdims=True)
    r2 = jnp.sum(jnp.where(oh2, before, 0.0), axis=-1, keepdims=True)
    cnt[...] += jnp.sum(oh, axis=0, keepdims=True)
    cnt_ref[...] = cnt[...]
    col = lax.broadcasted_iota(I32, (tm, 8), 1)
    out = jnp.zeros((tm, 8), F32)
    for j, v in enumerate((i1, i2, w1, w2, r1, r2)):
        out = jnp.where(col == j, v, out)
    o_ref[...] = out


def _router(x, w_router):
    t, d = x.shape
    ne = w_router.shape[1]
    tm = _tile(t, 512)
    return pl.pallas_call(
        _router_body,
        grid=(t // tm,),
        in_specs=[pl.BlockSpec((tm, d), lambda i: (i, 0)),
                  pl.BlockSpec((d, ne), lambda i: (0, 0))],
        out_specs=[pl.BlockSpec((tm, 8), lambda i: (i, 0)),
                   pl.BlockSpec((1, ne), lambda i: (0, 0))],
        out_shape=[jax.ShapeDtypeStruct((t, 8), F32), jax.ShapeDtypeStruct((1, ne), F32)],
        scratch_shapes=[pltpu.VMEM((1, ne), F32), pltpu.VMEM((tm, tm), BF16)],
        compiler_params=_cparams(32, 1),
        name="router",
    )(x, w_router)


ROW_DMA_GROUP = 8


def _start_row_gather(idx_ref, n_rows, src_hbm, dst, sem):
    def group(g, carry):
        for j in range(ROW_DMA_GROUP):
            r = g * ROW_DMA_GROUP + j
            row = idx_ref[0, 0, r]
            pltpu.make_async_copy(src_hbm.at[pl.ds(row, 1), :], dst.at[pl.ds(r, 1), :], sem).start()
        return carry
    lax.fori_loop(0, n_rows // ROW_DMA_GROUP, group, 0)


def _wait_rows(n_rows, hbm, vmem, sem):
    pltpu.make_async_copy(hbm.at[pl.ds(0, n_rows), :], vmem, sem).wait()


def _dispatch_body(off_ref, pad_ref, nv_ref, pos_ref, x_ref, xs_hbm, zbuf, sem):
    i = pl.program_id(0)
    tm = x_ref.shape[0]
    tz = zbuf.shape[0]

    @pl.when(i == 0)
    def _():
        zbuf[...] = jnp.zeros_like(zbuf)

        def zero_tile(row0):
            row0 = pl.multiple_of(row0, tz)
            cp = pltpu.make_async_copy(zbuf, xs_hbm.at[pl.ds(row0, tz), :], sem.at[1])
            cp.start()
            cp.wait()

        for e in range(off_ref.shape[0]):
            pl.when(pad_ref[e] > 0)(functools.partial(zero_tile, off_ref[e] + pad_ref[e] - tz))

        def tail(j, carry):
            zero_tile(j * tz)
            return carry
        lax.fori_loop(nv_ref[0], xs_hbm.shape[0] // tz, tail, 0)

    def group(g, carry):
        for j in range(ROW_DMA_GROUP):
            r = g * ROW_DMA_GROUP + j
            for slot in range(TOP_K):
                dst = pos_ref[0, 0, slot * tm + r]
                pltpu.make_async_copy(x_ref.at[pl.ds(r, 1), :], xs_hbm.at[pl.ds(dst, 1), :], sem.at[0]).start()
        return carry

    lax.fori_loop(0, tm // ROW_DMA_GROUP, group, 0)
    for _ in range(TOP_K):
        _wait_rows(tm, xs_hbm, x_ref, sem.at[0])


def _dispatch(x, pos_tiles, offsets, padded, n_valid, n_rows, tm, tm_e):
    t, d = x.shape
    grid_spec = pltpu.PrefetchScalarGridSpec(
        num_scalar_prefetch=3,
        grid=(t // tm,),
        in_specs=[pl.BlockSpec((1, 1, TOP_K * tm), lambda i, o, p, nv: (i, 0, 0), memory_space=pltpu.SMEM),
                  pl.BlockSpec((tm, d), lambda i, o, p, nv: (i, 0))],
        out_specs=pl.BlockSpec(memory_space=pl.ANY),
        scratch_shapes=[pltpu.VMEM((tm_e, d), F32), pltpu.SemaphoreType.DMA((2,))],
    )
    return pl.pallas_call(
        _dispatch_body,
        grid_spec=grid_spec,
        out_shape=jax.ShapeDtypeStruct((n_rows, d), F32),
        compiler_params=_cparams(32, 1),
        name="moe_dispatch",
    )(offsets, padded, n_valid, pos_tiles, x)


def _expert_body(te_ref, nv_ref, x_ref, wg_ref, wu_ref, wd_ref, o_ref):
    j = pl.program_id(0)

    @pl.when(j < nv_ref[0])
    def _():
        xb = x_ref[...].astype(BF16)
        g = _dot(xb, wg_ref[0])
        u = _dot(xb, wu_ref[0])
        o_ref[...] = _dot((_silu(g) * u).astype(BF16), wd_ref[0])

    @pl.when(j >= nv_ref[0])
    def _():
        o_ref[...] = jnp.zeros_like(o_ref)


def _experts(xs, tile_expert, n_valid, w_gate, w_up, w_down, tm):
    n_rows, d = xs.shape
    fdim = w_gate.shape[2]
    nt = n_rows // tm
    wspec = lambda shape: pl.BlockSpec((1,) + shape, lambda j, te, nv: (te[j], 0, 0))
    grid_spec = pltpu.PrefetchScalarGridSpec(
        num_scalar_prefetch=2,
        grid=(nt,),
        in_specs=[pl.BlockSpec((tm, d), lambda j, te, nv: (jnp.minimum(j, nv[0] - 1), 0)),
                  wspec((d, fdim)), wspec((d, fdim)), wspec((fdim, d))],
        out_specs=pl.BlockSpec((tm, d), lambda j, te, nv: (j, 0)),
    )
    return pl.pallas_call(
        _expert_body,
        grid_spec=grid_spec,
        out_shape=jax.ShapeDtypeStruct((n_rows, d), F32),
        compiler_params=_cparams(56, 1),
        name="experts",
    )(tile_expert, n_valid, xs, w_gate, w_up, w_down)


def _combine_body(pos_cur, pos_nxt, x_ref, route_ref, g_ref, b_ref, ys_hbm, o_ref, buf, sem):
    i = pl.program_id(0)
    n = pl.num_programs(0)
    tm = x_ref.shape[0]
    slot = i % 2

    @pl.when(i == 0)
    def _():
        _start_row_gather(pos_cur, TOP_K * tm, ys_hbm, buf.at[0], sem.at[0])

    @pl.when(i + 1 < n)
    def _():
        _start_row_gather(pos_nxt, TOP_K * tm, ys_hbm, buf.at[1 - slot], sem.at[1 - slot])

    _wait_rows(TOP_K * tm, ys_hbm, buf.at[slot], sem.at[slot])
    f = route_ref[:, 2:3] * buf[slot, 0:tm, :] + route_ref[:, 3:4] * buf[slot, tm:2 * tm, :]
    o_ref[...] = _layer_norm(ALPHA * x_ref[...] + f, g_ref[...], b_ref[...])


def _combine_ln(x, ys, route, pos, ln_g, ln_b, tm):
    t, d = x.shape
    n = t // tm
    pos_spec = lambda f: pl.BlockSpec((1, 1, TOP_K * tm), lambda i: (f(i), 0, 0), memory_space=pltpu.SMEM)
    vec = pl.BlockSpec((1, d), lambda i: (0, 0))
    return pl.pallas_call(
        _combine_body,
        grid=(n,),
        in_specs=[pos_spec(lambda i: i), pos_spec(lambda i: jnp.minimum(i + 1, n - 1)),
                  pl.BlockSpec((tm, d), lambda i: (i, 0)),
                  pl.BlockSpec((tm, route.shape[1]), lambda i: (i, 0)), vec, vec,
                  pl.BlockSpec(memory_space=pl.ANY)],
        out_specs=pl.BlockSpec((tm, d), lambda i: (i, 0)),
        out_shape=jax.ShapeDtypeStruct((t, d), F32),
        scratch_shapes=[pltpu.VMEM((2, TOP_K * tm, d), F32), pltpu.SemaphoreType.DMA((2,))],
        compiler_params=_cparams(40, 1),
        name="moe_combine_ln",
    )(pos, pos, x, route, ln_g.reshape(1, d), ln_b.reshape(1, d), ys)


def _moe_ln(x, w_router, w_gate, w_up, w_down, ln_g, ln_b):
    t, d = x.shape
    ne = w_router.shape[1]
    tm_e = _tile(t, 512)
    tm_c = _tile(t, 256)
    route, counts = _router(x, w_router)
    expert = route[:, 0:2].astype(I32)
    rank = route[:, 4:6].astype(I32)
    counts = counts[0].astype(I32)
    padded = ((counts + tm_e - 1) // tm_e) * tm_e
    ends = jnp.cumsum(padded)
    offsets = ends - padded
    pos = offsets[expert] + rank
    nt = (TOP_K * t) // tm_e + ne
    tile_expert = jnp.minimum(
        jnp.searchsorted(ends, jnp.arange(nt, dtype=I32) * tm_e, side="right"), ne - 1).astype(I32)
    n_valid = (ends[-1] // tm_e).astype(I32).reshape(1)
    tiles = lambda tm: pos.reshape(t // tm, tm, TOP_K).transpose(0, 2, 1).reshape(t // tm, 1, TOP_K * tm)
    xs = _dispatch(x, tiles(tm_c), offsets, padded, n_valid, nt * tm_e, tm_c, tm_e)
    ys = _experts(xs, tile_expert, n_valid, w_gate, w_up, w_down, tm_e)
    return _combine_ln(x, ys, route, tiles(tm_c), ln_g, ln_b, tm_c)


def _alibi_slopes(n_heads):
    return 2.0 ** (-8.0 * jnp.arange(1, n_heads + 1, dtype=F32) / n_heads)


def kernel(x_prompt, x_sample, l0_conv_w_in, l0_conv_b_in, l0_conv_w_dw, l0_conv_b_dw, l0_conv_norm_g, l0_conv_norm_b, l0_conv_w_out, l0_conv_b_out, l0_ln1_g, l0_ln1_b, l0_ffn_w_gate, l0_ffn_w_up, l0_ffn_w_down, l0_ln2_g, l0_ln2_b, l1_attn_w_qkv, l1_attn_lambda, l1_attn_subln_g, l1_attn_w_out, l1_ln1_g, l1_ln1_b, l1_moe_w_router, l1_moe_w_gate, l1_moe_w_up, l1_moe_w_down, l1_ln2_g, l1_ln2_b, l2_rec_w_in, l2_rec_b_in, l2_rec_conv_w, l2_rec_conv_b, l2_rec_gate_w, l2_rec_gate_b, l2_rec_lambda, l2_rec_w_out, l2_rec_b_out, l2_ln1_g, l2_ln1_b, l2_ffn_w_gate, l2_ffn_w_up, l2_ffn_w_down, l2_ln2_g, l2_ln2_b, l3_attn_w_qkv, l3_attn_sink, l3_attn_w_out, l3_ln1_g, l3_ln1_b, l3_moe_w_router, l3_moe_w_gate, l3_moe_w_up, l3_moe_w_down, l3_ln2_g, l3_ln2_b):
    d = x_prompt.shape[-1]
    bf = lambda w: w.astype(BF16)

    diff_w = l1_attn_w_qkv.shape[1] // 3
    n_diff = diff_w // (2 * DIFF_HEAD_DIM)
    col_scale = jnp.concatenate([jnp.full((diff_w,), LOG2E * DIFF_HEAD_DIM ** -0.5, F32), jnp.ones((2 * diff_w,), F32)])
    w_qkv1 = bf(l1_attn_w_qkv * col_scale)
    lam = l1_attn_lambda.astype(F32)
    lambda_init = 0.8 - 0.6 * math.exp(-0.3 * 1)
    lam_full = jnp.exp(jnp.sum(lam[0] * lam[1])) - jnp.exp(jnp.sum(lam[2] * lam[3])) + lambda_init
    slope2 = LOG2E * _alibi_slopes(n_diff)
    diff_scalars = jnp.concatenate(
        [lam_full.reshape(1),
         jnp.stack([*_split3_f32(slope2), *_split3_f32(DIFF_KEY_RADIX * slope2), slope2], axis=1).reshape(-1)])

    lru_w = l2_rec_w_out.shape[0]
    lru_left = l2_rec_conv_w.shape[0] // 2
    lru_c = -LRU_C * jax.nn.softplus(-l2_rec_lambda.astype(F32))
    gate_w = bf(l2_rec_gate_w)

    n_swa = d // SWA_HEAD_DIM
    swa_q = n_swa * SWA_HEAD_DIM
    col_scale3 = jnp.concatenate([jnp.full((swa_q,), LOG2E * SWA_HEAD_DIM ** -0.5, F32),
                                  jnp.ones((l3_attn_w_qkv.shape[1] - swa_q,), F32)])
    w_qkv3 = bf(l3_attn_w_qkv * col_scale3)
    swa_slopes = LOG2E * _alibi_slopes(n_swa)
    swa_sinks = LOG2E * l3_attn_sink.astype(F32)

    w_in0, w_out0 = bf(l0_conv_w_in), bf(l0_conv_w_out)
    ffn0 = (bf(l0_ffn_w_gate), bf(l0_ffn_w_up), bf(l0_ffn_w_down))
    w_out1 = bf(l1_attn_w_out)
    moe1 = (bf(l1_moe_w_gate), bf(l1_moe_w_up), bf(l1_moe_w_down))
    w_in2, w_out2 = bf(l2_rec_w_in), bf(l2_rec_w_out)
    ffn2 = (bf(l2_ffn_w_gate), bf(l2_ffn_w_up), bf(l2_ffn_w_down))
    w_out3 = bf(l3_attn_w_out)
    moe3 = (bf(l3_moe_w_gate), bf(l3_moe_w_up), bf(l3_moe_w_down))
    b_in0 = l0_conv_b_in.reshape(1, -1)
    b_in2 = l2_rec_b_in.reshape(1, -1)

    def trunk(x3):
        batch, seq, _ = x3.shape
        x = x3.reshape(batch * seq, d)
        glu = _mm_pair(x, w_in0, b_in0, mode="glu")
        x = _conv_out_ln(glu, x, l0_conv_w_dw, l0_conv_b_dw, l0_conv_norm_g, l0_conv_norm_b,
                         w_out0, l0_conv_b_out, l0_ln1_g, l0_ln1_b, seq)
        x = _ffn_ln(x, *ffn0, l0_ln2_g, l0_ln2_b)
        qkv = _mm(x, w_qkv1, BF16)
        o = _diff_attn(qkv, diff_scalars, l1_attn_subln_g, batch, seq, n_diff, 1.0 - lambda_init)
        x = _proj_ln([o], x, w_out1, None, l1_ln1_g, l1_ln1_b)
        x = _moe_ln(x, l1_moe_w_router, *moe1, l1_ln2_g, l1_ln2_b)
        y, r = _mm_pair(x, w_in2, b_in2, mode="gelu")
        r = r.reshape(batch, seq, lru_w)
        hs = [_lru_dir(r, l2_rec_conv_w, l2_rec_conv_b, gate_w[dr], l2_rec_gate_b[dr], lru_c[dr:dr + 1],
                       reverse=bool(dr), left=lru_left).reshape(batch * seq, lru_w) for dr in range(2)]
        x = _proj_ln([hs[0], hs[1], y], x, w_out2, l2_rec_b_out, l2_ln1_g, l2_ln1_b, mode="lru")
        x = _ffn_ln(x, *ffn2, l2_ln2_g, l2_ln2_b)
        qkv = _mm(x, w_qkv3, BF16)
        o = _win_attn(qkv, swa_slopes, swa_sinks, batch, seq, n_swa)
        x = _proj_ln([o], x, w_out3, None, l3_ln1_g, l3_ln1_b)
        x = _moe_ln(x, l3_moe_w_router, *moe3, l3_ln2_g, l3_ln2_b)
        return x.reshape(batch, seq, d)

    return (trunk(x_prompt), trunk(x_sample))
```

```python
import functools
import math

import jax
import jax.numpy as jnp
from jax import lax
from jax.experimental import pallas as pl
from jax.experimental.pallas import tpu as pltpu

F32 = jnp.float32
BF16 = jnp.bfloat16
I32 = jnp.int32

N_LAYERS = 4
LN_EPS = 1e-5
SUBLN_EPS = 1e-5
ALPHA = (2 * N_LAYERS) ** 0.25
LRU_C = 8.0
DIFF_HEAD_DIM = 128
SWA_HEAD_DIM = 128
SWA_KV_HEADS = 4
SWA_BLOCK = 128
TOP_K = 2
LOG2E = math.log2(math.e)
LANES = 128
HALO_ROWS = 16
LRU_HALO = 8
MIB = 1024 * 1024


def _cparams(vmem_mib, n_axes):
    return pltpu.CompilerParams(
        dimension_semantics=("arbitrary",) * n_axes, vmem_limit_bytes=vmem_mib * MIB)


def _tile(n, pref):
    t = min(n, pref)
    while n % t:
        t //= 2
    return t


def _sigmoid(x):
    return 0.5 * jnp.tanh(0.5 * x) + 0.5


def _silu(x):
    return x * _sigmoid(x)


def _gelu_tanh(x):
    return 0.5 * x * (1.0 + jnp.tanh(math.sqrt(2.0 / math.pi) * (x + 0.044715 * (x * x * x))))


def _dot(a, b):
    return jnp.dot(a, b, preferred_element_type=F32)


def _dot_nt(a, b):
    return lax.dot_general(a, b, (((1,), (1,)), ((), ())), preferred_element_type=F32)


def _layer_norm(y, g, b):
    mu = jnp.mean(y, axis=-1, keepdims=True)
    yc = y - mu
    var = jnp.mean(yc * yc, axis=-1, keepdims=True)
    return yc * lax.rsqrt(var + LN_EPS) * g + b


def _mm_body(x_ref, w_ref, o_ref, xb):
    @pl.when(pl.program_id(1) == 0)
    def _():
        xb[...] = x_ref[...].astype(BF16)

    o_ref[...] = _dot(xb[...], w_ref[...]).astype(o_ref.dtype)


def _mm(x, w, out_dtype):
    m, k = x.shape
    n = w.shape[1]
    tm, tn = _tile(m, 1024), _tile(n, 2048)
    return pl.pallas_call(
        _mm_body,
        grid=(m // tm, n // tn),
        in_specs=[pl.BlockSpec((tm, k), lambda i, j: (i, 0)),
                  pl.BlockSpec((k, tn), lambda i, j: (0, j))],
        out_specs=pl.BlockSpec((tm, tn), lambda i, j: (i, j)),
        out_shape=jax.ShapeDtypeStruct((m, n), out_dtype),
        scratch_shapes=[pltpu.VMEM((tm, k), BF16)],
        compiler_params=_cparams(56, 2),
        name="mm",
    )(x, w)


def _mm_pair_body(x_ref, wa_ref, wg_ref, ba_ref, bg_ref, *rest, mode):
    xb = rest[-1]

    @pl.when(pl.program_id(1) == 0)
    def _():
        xb[...] = x_ref[...].astype(BF16)

    a = _dot(xb[...], wa_ref[...]) + ba_ref[...]
    g = _dot(xb[...], wg_ref[...]) + bg_ref[...]
    if mode == "glu":
        rest[0][...] = a * _sigmoid(g)
    else:
        rest[0][...] = _gelu_tanh(a).astype(rest[0].dtype)
        rest[1][...] = g


def _mm_pair(x, w, bias, *, mode):
    m, k = x.shape
    n = w.shape[1] // 2
    tm, tn = _tile(m, 1024), _tile(n, 1024)
    nj = n // tn
    out_spec = pl.BlockSpec((tm, tn), lambda i, j: (i, j))
    if mode == "glu":
        out_specs, out_shape = out_spec, jax.ShapeDtypeStruct((m, n), F32)
    else:
        out_specs = [out_spec, out_spec]
        out_shape = [jax.ShapeDtypeStruct((m, n), BF16), jax.ShapeDtypeStruct((m, n), F32)]
    return pl.pallas_call(
        functools.partial(_mm_pair_body, mode=mode),
        grid=(m // tm, nj),
        in_specs=[pl.BlockSpec((tm, k), lambda i, j: (i, 0)),
                  pl.BlockSpec((k, tn), lambda i, j: (0, j)),
                  pl.BlockSpec((k, tn), lambda i, j: (0, j + nj)),
                  pl.BlockSpec((1, tn), lambda i, j: (0, j)),
                  pl.BlockSpec((1, tn), lambda i, j: (0, j + nj))],
        out_specs=out_specs,
        out_shape=out_shape,
        scratch_shapes=[pltpu.VMEM((tm, k), BF16)],
        compiler_params=_cparams(56, 2),
        name="mm_" + mode,
    )(x, w, w, bias, bias)


def _conv_out_body(cur_ref, prv_ref, nxt_ref, res_ref, wdw_ref, bdw_ref, ng_ref, nb_ref,
                   w_ref, bo_ref, g_ref, b_ref, o_ref, ext, cslab, lhs, *, seq, ts, width):
    i = pl.program_id(0)
    t0 = i * ts
    first = (t0 % seq) == 0
    last = ((t0 + ts) % seq) == 0
    nsl = cur_ref.shape[1] // LANES
    half = (width - 1) // 2
    for c in range(nsl):
        cs = slice(c * LANES, (c + 1) * LANES)
        ext[c, 0:HALO_ROWS, :] = jnp.where(first, 0.0, prv_ref[:, cs])
        ext[c, HALO_ROWS:HALO_ROWS + ts, :] = cur_ref[:, cs]
        ext[c, HALO_ROWS + ts:, :] = jnp.where(last, 0.0, nxt_ref[:, cs])

    def slab(c, carry):
        wk = wdw_ref[c]
        acc = jnp.broadcast_to(bdw_ref[c], (ts, LANES))
        for k in range(width):
            acc = acc + wk[k:k + 1, :] * ext[c, pl.ds(HALO_ROWS - half + k, ts), :]
        cslab[c] = acc
        return carry

    lax.fori_loop(0, nsl, slab, 0)

    s1 = cslab[0]
    for c in range(1, nsl):
        s1 = s1 + cslab[c]
    mu = jnp.sum(s1, axis=-1, keepdims=True) * (1.0 / (nsl * LANES))
    s2 = jnp.square(cslab[0] - mu)
    for c in range(1, nsl):
        s2 = s2 + jnp.square(cslab[c] - mu)
    var = jnp.sum(s2, axis=-1, keepdims=True) * (1.0 / (nsl * LANES))
    rstd = lax.rsqrt(var + LN_EPS)
    for c in range(nsl):
        cs = slice(c * LANES, (c + 1) * LANES)
        v = (cslab[c] - mu) * rstd * ng_ref[:, cs] + nb_ref[:, cs]
        lhs[:, cs] = _silu(v).astype(BF16)
    h = _dot(lhs[...], w_ref[...]) + bo_ref[...]
    o_ref[...] = _layer_norm(ALPHA * res_ref[...] + h, g_ref[...], b_ref[...])


def _conv_out_ln(glu, res, w_dw, b_dw, norm_g, norm_b, w_out, b_out, ln_g, ln_b, seq):
    t, d = glu.shape
    width = w_dw.shape[0]
    assert (width - 1) // 2 <= HALO_ROWS and d % LANES == 0
    ts = _tile(seq, 256)
    nsl = d // LANES
    hb = ts // HALO_ROWS
    wdw = w_dw.reshape(width, nsl, LANES).transpose(1, 0, 2)
    bdw = b_dw.reshape(nsl, 1, LANES)
    row = lambda v: v.reshape(1, d)
    vec = pl.BlockSpec((1, d), lambda i: (0, 0))
    return pl.pallas_call(
        functools.partial(_conv_out_body, seq=seq, ts=ts, width=width),
        grid=(t // ts,),
        in_specs=[pl.BlockSpec((ts, d), lambda i: (i, 0)),
                  pl.BlockSpec((HALO_ROWS, d), lambda i: (jnp.maximum(i * hb - 1, 0), 0)),
                  pl.BlockSpec((HALO_ROWS, d), lambda i: (jnp.minimum((i + 1) * hb, t // HALO_ROWS - 1), 0)),
                  pl.BlockSpec((ts, d), lambda i: (i, 0)),
                  pl.BlockSpec((nsl, width, LANES), lambda i: (0, 0, 0)),
                  pl.BlockSpec((nsl, 1, LANES), lambda i: (0, 0, 0)),
                  vec, vec,
                  pl.BlockSpec((d, d), lambda i: (0, 0)),
                  vec, vec, vec],
        out_specs=pl.BlockSpec((ts, d), lambda i: (i, 0)),
        out_shape=jax.ShapeDtypeStruct((t, d), F32),
        scratch_shapes=[pltpu.VMEM((nsl, ts + 2 * HALO_ROWS, LANES), F32),
                        pltpu.VMEM((nsl, ts, LANES), F32),
                        pltpu.VMEM((ts, d), BF16)],
        compiler_params=_cparams(48, 1),
        name="conv_out_ln",
    )(glu, glu, glu, res, wdw, bdw, row(norm_g), row(norm_b), w_out, row(b_out), row(ln_g), row(ln_b))


def _proj_ln_body(*refs, mode, has_bias):
    if mode == "lru":
        hf_ref, hb_ref, y_ref = refs[:3]
        rest = refs[3:]
        lhs = ((hf_ref[...] + hb_ref[...]) * y_ref[...].astype(F32)).astype(BF16)
    else:
        rest = refs[1:]
        lhs = refs[0][...].astype(BF16)
    if has_bias:
        res_ref, w_ref, bo_ref, g_ref, b_ref, o_ref = rest
    else:
        res_ref, w_ref, g_ref, b_ref, o_ref = rest
    h = _dot(lhs, w_ref[...])
    if has_bias:
        h = h + bo_ref[...]
    o_ref[...] = _layer_norm(ALPHA * res_ref[...] + h, g_ref[...], b_ref[...])


def _proj_ln(lhs_args, res, w, b_out, ln_g, ln_b, *, mode="plain"):
    t, d = res.shape
    k = w.shape[0]
    tm = _tile(t, 512)
    tok = lambda n: pl.BlockSpec((tm, n), lambda i: (i, 0))
    vec = pl.BlockSpec((1, d), lambda i: (0, 0))
    row = lambda v: v.reshape(1, d)
    in_specs = [tok(k) for _ in lhs_args] + [tok(d), pl.BlockSpec((k, d), lambda i: (0, 0),
                                                                  pipeline_mode=pl.Buffered(1))]
    args = list(lhs_args) + [res, w]
    if b_out is not None:
        in_specs.append(vec)
        args.append(row(b_out))
    in_specs += [vec, vec]
    args += [row(ln_g), row(ln_b)]
    return pl.pallas_call(
        functools.partial(_proj_ln_body, mode=mode, has_bias=b_out is not None),
        grid=(t // tm,),
        in_specs=in_specs,
        out_specs=tok(d),
        out_shape=jax.ShapeDtypeStruct((t, d), F32),
        compiler_params=_cparams(56, 1),
        name="proj_ln_" + mode,
    )(*args)


def _ffn_body(x_ref, wg_ref, wu_ref, wd_ref, g_ref, b_ref, o_ref, xb, acc):
    f = pl.program_id(1)

    @pl.when(f == 0)
    def _():
        xb[...] = x_ref[...].astype(BF16)
        acc[...] = jnp.zeros_like(acc)

    g = _dot(xb[...], wg_ref[...])
    u = _dot(xb[...], wu_ref[...])
    acc[...] += _dot((_silu(g) * u).astype(BF16), wd_ref[...])

    @pl.when(f == pl.num_programs(1) - 1)
    def _():
        o_ref[...] = _layer_norm(ALPHA * x_ref[...] + acc[...], g_ref[...], b_ref[...])


def _ffn_ln(x, w_gate, w_up, w_down, ln_g, ln_b):
    t, d = x.shape
    fdim = w_gate.shape[1]
    tm, tf = _tile(t, 512), _tile(fdim, 1024)
    vec = pl.BlockSpec((1, d), lambda i, f: (0, 0))
    return pl.pallas_call(
        _ffn_body,
        grid=(t // tm, fdim // tf),
        in_specs=[pl.BlockSpec((tm, d), lambda i, f: (i, 0)),
                  pl.BlockSpec((d, tf), lambda i, f: (0, f)),
                  pl.BlockSpec((d, tf), lambda i, f: (0, f)),
                  pl.BlockSpec((tf, d), lambda i, f: (f, 0)),
                  vec, vec],
        out_specs=pl.BlockSpec((tm, d), lambda i, f: (i, 0)),
        out_shape=jax.ShapeDtypeStruct((t, d), F32),
        scratch_shapes=[pltpu.VMEM((tm, d), BF16), pltpu.VMEM((tm, d), F32)],
        compiler_params=_cparams(56, 2),
        name="ffn_ln",
    )(x, w_gate, w_up, w_down, ln_g.reshape(1, d), ln_b.reshape(1, d))


DIFF_KEY_RADIX = 256
DIFF_SC_PER_HEAD = 7


def _split3_f32(x):
    hi = x.astype(BF16).astype(F32)
    mid = (x - hi).astype(BF16).astype(F32)
    lo = (x - hi - mid).astype(BF16).astype(F32)
    return hi, mid, lo


def _lane_tiles(x, op):
    return functools.reduce(op, [x[:, t * LANES:(t + 1) * LANES] for t in range(x.shape[1] // LANES)])


def _diff_attn_body(sc_ref, q_ref, k_ref, v_ref, kx_ref, sg_ref, o_ref, qa, sfull, oacc, m128, l128,
                    *, seq, tq, nqt, out_scale):
    hd = DIFF_HEAD_DIM
    tk = tq
    nk = seq // tk
    h = pl.program_id(1)
    lam = sc_ref[0]
    base = 1 + DIFF_SC_PER_HEAD * h
    slope2 = sc_ref[base + 6]

    irel = lax.broadcasted_iota(I32, (tq, 1), 0).astype(F32)
    lane = lax.broadcasted_iota(I32, (tq, hd), 1)
    ext = jnp.zeros((tq, hd), F32)
    for j, v in enumerate((*_split3_f32(-slope2 * irel), *[sc_ref[base + t] for t in range(6)])):
        ext = jnp.where(lane == j, v, ext)
    for qt in range(nqt):
        for c in range(2):
            qc = q_ref[qt * tq:(qt + 1) * tq, c * hd:(c + 1) * hd]
            qa[qt, c, 0] = jnp.concatenate([qc, ext.astype(BF16)], axis=1)
            qa[qt, c, 1] = jnp.concatenate([qc, (-ext).astype(BF16)], axis=1)
    d = (lax.broadcasted_iota(I32, (tq, tk), 0) - lax.broadcasted_iota(I32, (tq, tk), 1)).astype(F32)
    fix = (2.0 * slope2) * jnp.minimum(d, 0.0)
    m128[...] = jnp.full_like(m128, -jnp.inf)
    l128[...] = jnp.zeros_like(l128)
    oacc[...] = jnp.zeros_like(oacc)

    def tile(qt, r):
        iq = pl.program_id(2) * nqt + qt
        kt = lax.rem(iq + r, nk)
        j0 = pl.multiple_of(kt * tk, tk)
        if r == 0:
            return j0, 0, 0.0
        c0 = ((iq - kt) * tq).astype(F32)
        right = kt > iq
        return j0, right.astype(I32), jnp.where(right, slope2 * c0, -slope2 * c0)

    def scores(qt, r):
        j0, sgn, kappa = tile(qt, r)
        kk = k_ref[pl.ds(j0, tk), :]
        for c in range(2):
            ka = jnp.concatenate([kk[:, c * hd:(c + 1) * hd], kx_ref[...]], axis=1)
            s = _dot_nt(qa[qt, c, sgn], ka)
            if r == 0:
                s = s + fix
            sfull[qt, c, :, r * tk:(r + 1) * tk] = s
            m128[qt, c] = jnp.maximum(m128[qt, c], _lane_tiles(s, jnp.maximum) + kappa)

    def values(qt, r, mrow):
        j0, _, kappa = tile(qt, r)
        vv = v_ref[pl.ds(j0, tk), :]
        for c in range(2):
            p = jnp.exp2(sfull[qt, c, :, r * tk:(r + 1) * tk] - (mrow[c] - kappa))
            l128[qt, c] += _lane_tiles(p, jnp.add)
            oacc[qt, c] += _dot(p.astype(BF16), vv)

    def row_max(qt):
        return [jnp.max(m128[qt, c], axis=-1, keepdims=True) for c in range(2)]

    def finish(qt):
        lsum = [jnp.sum(l128[qt, c], axis=-1, keepdims=True) for c in range(2)]
        o = oacc[qt, 0] / lsum[0] - lam * (oacc[qt, 1] / lsum[1])
        o = o * lax.rsqrt(jnp.mean(o * o, axis=-1, keepdims=True) + SUBLN_EPS) * (sg_ref[...] * out_scale)
        o_ref[qt * tq:(qt + 1) * tq, :] = o.astype(o_ref.dtype)

    for r in range(nk):
        scores(0, r)
    for qt in range(nqt):
        mrow = row_max(qt)
        for r in range(nk):
            if qt + 1 < nqt:
                scores(qt + 1, r)
            values(qt, r, mrow)
        finish(qt)


def _diff_attn(qkv, scalars, subln_g, batch, seq, n_heads, out_scale):
    t = qkv.shape[0]
    hd = DIFF_HEAD_DIM
    vd = 2 * hd
    tq = _tile(seq, 512)
    nqt = 2 if (seq // tq) % 2 == 0 else 1
    assert tq <= DIFF_KEY_RADIX * DIFF_KEY_RADIX
    nq = seq // (tq * nqt)
    lane = jnp.arange(hd)[None, :]
    jrel = jnp.arange(tq)[:, None]
    jl = (jrel % DIFF_KEY_RADIX).astype(F32)
    jh = (jrel // DIFF_KEY_RADIX).astype(F32)
    kx = jnp.where(lane < 3, 1.0, jnp.where(lane < 6, jl, jnp.where(lane < 9, jh, 0.0))).astype(BF16)
    return pl.pallas_call(
        functools.partial(_diff_attn_body, seq=seq, tq=tq, nqt=nqt, out_scale=out_scale),
        grid=(batch, n_heads, nq),
        in_specs=[pl.BlockSpec(memory_space=pltpu.SMEM),
                  pl.BlockSpec((nqt * tq, vd), lambda b, h, i: (b * nq + i, h)),
                  pl.BlockSpec((seq, vd), lambda b, h, i: (b, n_heads + h)),
                  pl.BlockSpec((seq, vd), lambda b, h, i: (b, 2 * n_heads + h)),
                  pl.BlockSpec((tq, hd), lambda b, h, i: (0, 0)),
                  pl.BlockSpec((1, vd), lambda b, h, i: (0, 0))],
        out_specs=pl.BlockSpec((nqt * tq, vd), lambda b, h, i: (b * nq + i, h)),
        out_shape=jax.ShapeDtypeStruct((t, n_heads * vd), BF16),
        scratch_shapes=[pltpu.VMEM((nqt, 2, 2, tq, vd), BF16), pltpu.VMEM((nqt, 2, tq, seq), F32),
                        pltpu.VMEM((nqt, 2, tq, vd), F32), pltpu.VMEM((nqt, 2, tq, LANES), F32),
                        pltpu.VMEM((nqt, 2, tq, LANES), F32)],
        compiler_params=_cparams(58, 3),
        name="diff_attn",
    )(scalars, qkv, qkv, qkv, kx, subln_g.reshape(1, vd))


def _win_attn_body(sc_ref, bias_ref, q_ref, kp_ref, kc_ref, kn_ref, vp_ref, vc_ref, vn_ref, o_ref,
                   *, n_heads):
    d, blk = SWA_HEAD_DIM, SWA_BLOCK
    grp = n_heads // SWA_KV_HEADS
    rows = grp * blk
    head_of_row = lax.broadcasted_iota(I32, (rows, 1), 0) // blk
    for kv in range(SWA_KV_HEADS):
        cs = slice(kv * d, (kv + 1) * d)
        kw = jnp.concatenate([kp_ref[:, cs], kc_ref[:, cs], kn_ref[:, cs]], axis=0)
        vw = jnp.concatenate([vp_ref[:, cs], vc_ref[:, cs], vn_ref[:, cs]], axis=0)
        qg = jnp.concatenate([q_ref[:, (kv * grp + g) * d:(kv * grp + g + 1) * d] for g in range(grp)], axis=0)
        sink = jnp.zeros((rows, 1), F32)
        for g in range(grp):
            sink = jnp.where(head_of_row == g, sc_ref[kv * grp + g], sink)
        s = _dot_nt(qg, kw) - bias_ref[0, kv]
        m = jnp.maximum(jnp.max(s, axis=-1, keepdims=True), sink)
        p = jnp.exp2(s - m)
        denom = jnp.sum(p, axis=-1, keepdims=True) + jnp.exp2(sink - m)
        o = _dot(p.astype(BF16), vw) / denom
        for g in range(grp):
            hcol = (kv * grp + g) * d
            o_ref[:, hcol:hcol + d] = o[g * blk:(g + 1) * blk, :].astype(o_ref.dtype)


def _win_attn(qkv, slopes, sinks, batch, seq, n_heads):
    t = qkv.shape[0]
    d, blk = SWA_HEAD_DIM, SWA_BLOCK
    nb = seq // blk
    qw, kvw = n_heads * d, SWA_KV_HEADS * d
    assert qw % kvw == 0 and nb >= 2
    grp = n_heads // SWA_KV_HEADS
    rows = grp * blk
    qrow = jnp.arange(rows)[:, None] % blk
    col = jnp.arange(3 * blk)[None, :]
    rel = jnp.abs(qrow - col + blk)
    band = rel <= blk
    valid = jnp.stack([band & (col >= blk), band, band & (col < 2 * blk)])
    slope_rows = jnp.repeat(slopes.reshape(SWA_KV_HEADS, grp), blk, axis=1)
    bias = jnp.where(valid[:, None], slope_rows[None, :, :, None] * rel.astype(F32)[None, None], jnp.inf)
    edge = lambda n: jnp.where(n == 0, 0, jnp.where(n == nb - 1, 2, 1))
    kcol, vcol = qw // kvw, qw // kvw + 1
    prev = lambda b, n: b * nb + jnp.maximum(n - 1, 0)
    nxt = lambda b, n: b * nb + jnp.minimum(n + 1, nb - 1)
    cur = lambda b, n: b * nb + n
    kvspec = lambda f, col: pl.BlockSpec((blk, kvw), lambda b, n: (f(b, n), col))
    return pl.pallas_call(
        functools.partial(_win_attn_body, n_heads=n_heads),
        grid=(batch, nb),
        in_specs=[pl.BlockSpec(memory_space=pltpu.SMEM),
                  pl.BlockSpec((1, SWA_KV_HEADS, rows, 3 * blk), lambda b, n: (edge(n), 0, 0, 0)),
                  pl.BlockSpec((blk, qw), lambda b, n: (cur(b, n), 0)),
                  kvspec(prev, kcol), kvspec(cur, kcol), kvspec(nxt, kcol),
                  kvspec(prev, vcol), kvspec(cur, vcol), kvspec(nxt, vcol)],
        out_specs=pl.BlockSpec((blk, qw), lambda b, n: (cur(b, n), 0)),
        out_shape=jax.ShapeDtypeStruct((t, qw), BF16),
        compiler_params=_cparams(32, 2),
        name="win_attn",
    )(sinks, bias, qkv, qkv, qkv, qkv, qkv, qkv, qkv)


def _lru_body(r_hbm, cw_ref, cb_ref, gw_ref, gb_ref, c_ref, h_hbm,
              rbuf, hout, abuf, bbuf, hcar, isem, osem, *, seq, ts, batch, reverse, left):
    i = pl.program_id(0)
    nch = seq // ts
    taps = cw_ref.shape[0]
    right = taps - 1 - left
    hal = LRU_HALO
    width = r_hbm.shape[2]
    nblk, bw = gw_ref.shape[1], gw_ref.shape[2]
    chunk = lambda s: (nch - 1 - s) if reverse else s
    slot = i % 2

    def in_copies(ci, sl, op):
        t0 = ci * ts

        def each(src_start, dst_start, rows):
            src_start = pl.multiple_of(src_start, LRU_HALO)
            for b in range(batch):
                cp = pltpu.make_async_copy(r_hbm.at[b, pl.ds(src_start, rows), :],
                                           rbuf.at[sl, pl.ds(dst_start, rows), b, :], isem.at[sl])
                getattr(cp, op)()

        each(t0, hal, ts)
        pl.when(ci > 0)(lambda: each(jnp.maximum(t0 - hal, 0), 0, hal))
        pl.when(ci < nch - 1)(lambda: each(jnp.minimum(t0 + ts, seq - hal), hal + ts, hal))

    def out_copies(ci, sl, op):
        for b in range(batch):
            cp = pltpu.make_async_copy(hout.at[sl, pl.ds(0, ts), b, :],
                                       h_hbm.at[b, pl.ds(ci * ts, ts), :], osem.at[sl])
            getattr(cp, op)()

    @pl.when(i == 0)
    def _():
        hcar[...] = jnp.zeros_like(hcar)
        in_copies(chunk(0), 0, "start")

    @pl.when(i + 1 < nch)
    def _():
        in_copies(chunk(i + 1), 1 - slot, "start")

    ci = chunk(i)
    in_copies(ci, slot, "wait")

    @pl.when(ci == 0)
    def _():
        rbuf[slot, hal - left:hal] = jnp.zeros((left, batch, width), F32)

    @pl.when(ci == nch - 1)
    def _():
        rbuf[slot, hal + ts:hal + ts + right] = jnp.zeros((right, batch, width), F32)

    rf = jnp.broadcast_to(cb_ref[...], (ts, batch, width))
    for k in range(taps):
        rf = rf + cw_ref[k:k + 1, :] * rbuf[slot, pl.ds(hal - left + k, ts)]
    rf2 = rf.reshape(ts * batch, width)
    for nbk in range(nblk):
        cs = slice(nbk * bw, (nbk + 1) * bw)
        xb = rf2[:, cs]
        xbb = xb.astype(BF16)
        tr = jnp.tanh(_dot(xbb, gw_ref[0, nbk]) + gb_ref[0:1, cs])
        ti = jnp.tanh(_dot(xbb, gw_ref[1, nbk]) + gb_ref[1:2, cs])
        a = jnp.exp2(c_ref[:, cs] * tr + c_ref[:, cs])
        hx = 0.5 * xb
        bb = jnp.sqrt(1.0 - a * a) * (hx * ti + hx)
        abuf[:, :, cs] = a.reshape(ts, batch, bw)
        bbuf[:, :, cs] = bb.reshape(ts, batch, bw)

    @pl.when(i >= 2)
    def _():
        out_copies(ci, slot, "wait")

    def step(s, h):
        t = (ts - 1 - s) if reverse else s
        h = abuf[t] * h + bbuf[t]
        hout[slot, t] = h
        return h

    hcar[...] = lax.fori_loop(0, ts, step, hcar[...])
    out_copies(ci, slot, "start")

    @pl.when(i == nch - 1)
    def _():
        out_copies(ci, slot, "wait")
        if nch >= 2:
            out_copies(ci, 1 - slot, "wait")


def _lru_dir(r3, conv_w, conv_b, gate_w, gate_b, cvec, *, reverse, left):
    batch, seq, width = r3.shape
    ts = _tile(seq, 512 // batch)
    taps = conv_w.shape[0]
    assert batch % 8 == 0 and ts % LRU_HALO == 0 and max(left, taps - 1 - left) <= LRU_HALO
    full = lambda a: pl.BlockSpec(a.shape, lambda i: (0,) * a.ndim)
    cb = conv_b.reshape(1, width)
    return pl.pallas_call(
        functools.partial(_lru_body, seq=seq, ts=ts, batch=batch, reverse=reverse, left=left),
        grid=(seq // ts,),
        in_specs=[pl.BlockSpec(memory_space=pl.ANY), full(conv_w), full(cb), full(gate_w),
                  full(gate_b), full(cvec)],
        out_specs=pl.BlockSpec(memory_space=pl.ANY),
        out_shape=jax.ShapeDtypeStruct((batch, seq, width), F32),
        scratch_shapes=[pltpu.VMEM((2, ts + 2 * LRU_HALO, batch, width), F32),
                        pltpu.VMEM((2, ts, batch, width), F32),
                        pltpu.VMEM((ts, batch, width), F32),
                        pltpu.VMEM((ts, batch, width), F32),
                        pltpu.VMEM((batch, width), F32),
                        pltpu.SemaphoreType.DMA((2,)),
                        pltpu.SemaphoreType.DMA((2,))],
        compiler_params=_cparams(48, 1),
        name="lru_rev" if reverse else "lru_fwd",
    )(r3, conv_w, cb, gate_w, gate_b, cvec)


def _split_bf16(x):
    hi = x.astype(BF16)
    lo = (x - hi.astype(F32)).astype(BF16)
    return hi, lo


def _router_body(x_ref, wr_ref, o_ref, cnt_ref, cnt, lower):
    i = pl.program_id(0)
    tm = x_ref.shape[0]
    ne = wr_ref.shape[1]

    @pl.when(i == 0)
    def _():
        cnt[...] = jnp.zeros_like(cnt)
        lower[...] = jnp.where(lax.broadcasted_iota(I32, (tm, tm), 1) < lax.broadcasted_iota(I32, (tm, tm), 0),
                               1.0, 0.0).astype(BF16)

    xh, xl = _split_bf16(x_ref[...])
    wh, wl = _split_bf16(wr_ref[...])
    logits = _dot(xh, wh) + _dot(xl, wh) + _dot(xh, wl)
    lane = lax.broadcasted_iota(I32, (tm, ne), 1).astype(F32)
    v1 = jnp.max(logits, axis=-1, keepdims=True)
    i1 = jnp.min(jnp.where(logits == v1, lane, float(ne)), axis=-1, keepdims=True)
    rest = jnp.where(lane == i1, -jnp.inf, logits)
    v2 = jnp.max(rest, axis=-1, keepdims=True)
    i2 = jnp.min(jnp.where(rest == v2, lane, float(ne)), axis=-1, keepdims=True)
    e2 = jnp.exp(v2 - v1)
    w1 = 1.0 / (1.0 + e2)
    w2 = e2 / (1.0 + e2)
    oh1 = lane == i1
    oh2 = lane == i2
    oh = jnp.where(oh1 | oh2, 1.0, 0.0)
    before = _dot(lower[...], oh.astype(BF16)) + cnt[...]
    r1 = jnp.sum(jnp.where(oh1, before, 0.0), axis=-1, keepdims=True)
    r2 = jnp.sum(jnp.where(oh2, before, 0.0), axis=-1, keepdims=True)
    cnt[...] += jnp.sum(oh, axis=0, keepdims=True)
    cnt_ref[...] = cnt[...]
    col = lax.broadcasted_iota(I32, (tm, 8), 1)
    out = jnp.zeros((tm, 8), F32)
    for j, v in enumerate((i1, i2, w1, w2, r1, r2)):
        out = jnp.where(col == j, v, out)
    o_ref[...] = out


def _router(x, w_router):
    t, d = x.shape
    ne = w_router.shape[1]
    tm = _tile(t, 512)
    return pl.pallas_call(
        _router_body,
        grid=(t // tm,),
        in_specs=[pl.BlockSpec((tm, d), lambda i: (i, 0)),
                  pl.BlockSpec((d, ne), lambda i: (0, 0))],
        out_specs=[pl.BlockSpec((tm, 8), lambda i: (i, 0)),
                   pl.BlockSpec((1, ne), lambda i: (0, 0))],
        out_shape=[jax.ShapeDtypeStruct((t, 8), F32), jax.ShapeDtypeStruct((1, ne), F32)],
        scratch_shapes=[pltpu.VMEM((1, ne), F32), pltpu.VMEM((tm, tm), BF16)],
        compiler_params=_cparams(32, 1),
        name="router",
    )(x, w_router)


ROW_DMA_GROUP = 8


def _start_row_gather(idx_ref, n_rows, src_hbm, dst, sem):
    def group(g, carry):
        for j in range(ROW_DMA_GROUP):
            r = g * ROW_DMA_GROUP + j
            row = idx_ref[0, 0, r]
            pltpu.make_async_copy(src_hbm.at[pl.ds(row, 1), :], dst.at[pl.ds(r, 1), :], sem).start()
        return carry
    lax.fori_loop(0, n_rows // ROW_DMA_GROUP, group, 0)


def _wait_rows(n_rows, hbm, vmem, sem):
    pltpu.make_async_copy(hbm.at[pl.ds(0, n_rows), :], vmem, sem).wait()


def _dispatch_body(off_ref, pad_ref, nv_ref, pos_ref, x_ref, xs_hbm, zbuf, sem):
    i = pl.program_id(0)
    tm = x_ref.shape[0]
    tz = zbuf.shape[0]

    @pl.when(i == 0)
    def _():
        zbuf[...] = jnp.zeros_like(zbuf)

        def zero_tile(row0):
            row0 = pl.multiple_of(row0, tz)
            cp = pltpu.make_async_copy(zbuf, xs_hbm.at[pl.ds(row0, tz), :], sem.at[1])
            cp.start()
            cp.wait()

        for e in range(off_ref.shape[0]):
            pl.when(pad_ref[e] > 0)(functools.partial(zero_tile, off_ref[e] + pad_ref[e] - tz))

        def tail(j, carry):
            zero_tile(j * tz)
            return carry
        lax.fori_loop(nv_ref[0], xs_hbm.shape[0] // tz, tail, 0)

    def group(g, carry):
        for j in range(ROW_DMA_GROUP):
            r = g * ROW_DMA_GROUP + j
            for slot in range(TOP_K):
                dst = pos_ref[0, 0, slot * tm + r]
                pltpu.make_async_copy(x_ref.at[pl.ds(r, 1), :], xs_hbm.at[pl.ds(dst, 1), :], sem.at[0]).start()
        return carry

    lax.fori_loop(0, tm // ROW_DMA_GROUP, group, 0)
    for _ in range(TOP_K):
        _wait_rows(tm, xs_hbm, x_ref, sem.at[0])


def _dispatch(x, pos_tiles, offsets, padded, n_valid, n_rows, tm, tm_e):
    t, d = x.shape
    grid_spec = pltpu.PrefetchScalarGridSpec(
        num_scalar_prefetch=3,
        grid=(t // tm,),
        in_specs=[pl.BlockSpec((1, 1, TOP_K * tm), lambda i, o, p, nv: (i, 0, 0), memory_space=pltpu.SMEM),
                  pl.BlockSpec((tm, d), lambda i, o, p, nv: (i, 0))],
        out_specs=pl.BlockSpec(memory_space=pl.ANY),
        scratch_shapes=[pltpu.VMEM((tm_e, d), F32), pltpu.SemaphoreType.DMA((2,))],
    )
    return pl.pallas_call(
        _dispatch_body,
        grid_spec=grid_spec,
        out_shape=jax.ShapeDtypeStruct((n_rows, d), F32),
        compiler_params=_cparams(32, 1),
        name="moe_dispatch",
    )(offsets, padded, n_valid, pos_tiles, x)


def _expert_body(te_ref, nv_ref, x_ref, wg_ref, wu_ref, wd_ref, o_ref):
    j = pl.program_id(0)

    @pl.when(j < nv_ref[0])
    def _():
        xb = x_ref[...].astype(BF16)
        g = _dot(xb, wg_ref[0])
        u = _dot(xb, wu_ref[0])
        o_ref[...] = _dot((_silu(g) * u).astype(BF16), wd_ref[0])

    @pl.when(j >= nv_ref[0])
    def _():
        o_ref[...] = jnp.zeros_like(o_ref)


def _experts(xs, tile_expert, n_valid, w_gate, w_up, w_down, tm):
    n_rows, d = xs.shape
    fdim = w_gate.shape[2]
    nt = n_rows // tm
    wspec = lambda shape: pl.BlockSpec((1,) + shape, lambda j, te, nv: (te[j], 0, 0))
    grid_spec = pltpu.PrefetchScalarGridSpec(
        num_scalar_prefetch=2,
        grid=(nt,),
        in_specs=[pl.BlockSpec((tm, d), lambda j, te, nv: (jnp.minimum(j, nv[0] - 1), 0)),
                  wspec((d, fdim)), wspec((d, fdim)), wspec((fdim, d))],
        out_specs=pl.BlockSpec((tm, d), lambda j, te, nv: (j, 0)),
    )
    return pl.pallas_call(
        _expert_body,
        grid_spec=grid_spec,
        out_shape=jax.ShapeDtypeStruct((n_rows, d), F32),
        compiler_params=_cparams(56, 1),
        name="experts",
    )(tile_expert, n_valid, xs, w_gate, w_up, w_down)


def _combine_body(pos_cur, pos_nxt, x_ref, route_ref, g_ref, b_ref, ys_hbm, o_ref, buf, sem):
    i = pl.program_id(0)
    n = pl.num_programs(0)
    tm = x_ref.shape[0]
    slot = i % 2

    @pl.when(i == 0)
    def _():
        _start_row_gather(pos_cur, TOP_K * tm, ys_hbm, buf.at[0], sem.at[0])

    @pl.when(i + 1 < n)
    def _():
        _start_row_gather(pos_nxt, TOP_K * tm, ys_hbm, buf.at[1 - slot], sem.at[1 - slot])

    _wait_rows(TOP_K * tm, ys_hbm, buf.at[slot], sem.at[slot])
    f = route_ref[:, 2:3] * buf[slot, 0:tm, :] + route_ref[:, 3:4] * buf[slot, tm:2 * tm, :]
    o_ref[...] = _layer_norm(ALPHA * x_ref[...] + f, g_ref[...], b_ref[...])


def _combine_ln(x, ys, route, pos, ln_g, ln_b, tm):
    t, d = x.shape
    n = t // tm
    pos_spec = lambda f: pl.BlockSpec((1, 1, TOP_K * tm), lambda i: (f(i), 0, 0), memory_space=pltpu.SMEM)
    vec = pl.BlockSpec((1, d), lambda i: (0, 0))
    return pl.pallas_call(
        _combine_body,
        grid=(n,),
        in_specs=[pos_spec(lambda i: i), pos_spec(lambda i: jnp.minimum(i + 1, n - 1)),
                  pl.BlockSpec((tm, d), lambda i: (i, 0)),
                  pl.BlockSpec((tm, route.shape[1]), lambda i: (i, 0)), vec, vec,
                  pl.BlockSpec(memory_space=pl.ANY)],
        out_specs=pl.BlockSpec((tm, d), lambda i: (i, 0)),
        out_shape=jax.ShapeDtypeStruct((t, d), F32),
        scratch_shapes=[pltpu.VMEM((2, TOP_K * tm, d), F32), pltpu.SemaphoreType.DMA((2,))],
        compiler_params=_cparams(40, 1),
        name="moe_combine_ln",
    )(pos, pos, x, route, ln_g.reshape(1, d), ln_b.reshape(1, d), ys)


def _moe_ln(x, w_router, w_gate, w_up, w_down, ln_g, ln_b):
    t, d = x.shape
    ne = w_router.shape[1]
    tm_e = _tile(t, 512)
    tm_c = _tile(t, 256)
    route, counts = _router(x, w_router)
    expert = route[:, 0:2].astype(I32)
    rank = route[:, 4:6].astype(I32)
    counts = counts[0].astype(I32)
    padded = ((counts + tm_e - 1) // tm_e) * tm_e
    ends = jnp.cumsum(padded)
    offsets = ends - padded
    pos = offsets[expert] + rank
    nt = (TOP_K * t) // tm_e + ne
    tile_expert = jnp.minimum(
        jnp.searchsorted(ends, jnp.arange(nt, dtype=I32) * tm_e, side="right"), ne - 1).astype(I32)
    n_valid = (ends[-1] // tm_e).astype(I32).reshape(1)
    tiles = lambda tm: pos.reshape(t // tm, tm, TOP_K).transpose(0, 2, 1).reshape(t // tm, 1, TOP_K * tm)
    xs = _dispatch(x, tiles(tm_c), offsets, padded, n_valid, nt * tm_e, tm_c, tm_e)
    ys = _experts(xs, tile_expert, n_valid, w_gate, w_up, w_down, tm_e)
    return _combine_ln(x, ys, route, tiles(tm_c), ln_g, ln_b, tm_c)


def _alibi_slopes(n_heads):
    return 2.0 ** (-8.0 * jnp.arange(1, n_heads + 1, dtype=F32) / n_heads)


def kernel(x_prompt, x_sample, l0_conv_w_in, l0_conv_b_in, l0_conv_w_dw, l0_conv_b_dw, l0_conv_norm_g, l0_conv_norm_b, l0_conv_w_out, l0_conv_b_out, l0_ln1_g, l0_ln1_b, l0_ffn_w_gate, l0_ffn_w_up, l0_ffn_w_down, l0_ln2_g, l0_ln2_b, l1_attn_w_qkv, l1_attn_lambda, l1_attn_subln_g, l1_attn_w_out, l1_ln1_g, l1_ln1_b, l1_moe_w_router, l1_moe_w_gate, l1_moe_w_up, l1_moe_w_down, l1_ln2_g, l1_ln2_b, l2_rec_w_in, l2_rec_b_in, l2_rec_conv_w, l2_rec_conv_b, l2_rec_gate_w, l2_rec_gate_b, l2_rec_lambda, l2_rec_w_out, l2_rec_b_out, l2_ln1_g, l2_ln1_b, l2_ffn_w_gate, l2_ffn_w_up, l2_ffn_w_down, l2_ln2_g, l2_ln2_b, l3_attn_w_qkv, l3_attn_sink, l3_attn_w_out, l3_ln1_g, l3_ln1_b, l3_moe_w_router, l3_moe_w_gate, l3_moe_w_up, l3_moe_w_down, l3_ln2_g, l3_ln2_b):
    d = x_prompt.shape[-1]
    bf = lambda w: w.astype(BF16)

    diff_w = l1_attn_w_qkv.shape[1] // 3
    n_diff = diff_w // (2 * DIFF_HEAD_DIM)
    col_scale = jnp.concatenate([jnp.full((diff_w,), LOG2E * DIFF_HEAD_DIM ** -0.5, F32), jnp.ones((2 * diff_w,), F32)])
    w_qkv1 = bf(l1_attn_w_qkv * col_scale)
    lam = l1_attn_lambda.astype(F32)
    lambda_init = 0.8 - 0.6 * math.exp(-0.3 * 1)
    lam_full = jnp.exp(jnp.sum(lam[0] * lam[1])) - jnp.exp(jnp.sum(lam[2] * lam[3])) + lambda_init
    slope2 = LOG2E * _alibi_slopes(n_diff)
    diff_scalars = jnp.concatenate(
        [lam_full.reshape(1),
         jnp.stack([*_split3_f32(slope2), *_split3_f32(DIFF_KEY_RADIX * slope2), slope2], axis=1).reshape(-1)])

    lru_w = l2_rec_w_out.shape[0]
    lru_left = l2_rec_conv_w.shape[0] // 2
    lru_c = (-0.5 * LOG2E * LRU_C) * jax.nn.softplus(-l2_rec_lambda.astype(F32))
    gate_w = bf(0.5 * l2_rec_gate_w)
    gate_b = 0.5 * l2_rec_gate_b

    n_swa = d // SWA_HEAD_DIM
    swa_q = n_swa * SWA_HEAD_DIM
    col_scale3 = jnp.concatenate([jnp.full((swa_q,), LOG2E * SWA_HEAD_DIM ** -0.5, F32),
                                  jnp.ones((l3_attn_w_qkv.shape[1] - swa_q,), F32)])
    w_qkv3 = bf(l3_attn_w_qkv * col_scale3)
    swa_slopes = LOG2E * _alibi_slopes(n_swa)
    swa_sinks = LOG2E * l3_attn_sink.astype(F32)

    w_in0, w_out0 = bf(l0_conv_w_in), bf(l0_conv_w_out)
    ffn0 = (bf(l0_ffn_w_gate), bf(l0_ffn_w_up), bf(l0_ffn_w_down))
    w_out1 = bf(l1_attn_w_out)
    moe1 = (bf(l1_moe_w_gate), bf(l1_moe_w_up), bf(l1_moe_w_down))
    w_in2, w_out2 = bf(l2_rec_w_in), bf(l2_rec_w_out)
    ffn2 = (bf(l2_ffn_w_gate), bf(l2_ffn_w_up), bf(l2_ffn_w_down))
    w_out3 = bf(l3_attn_w_out)
    moe3 = (bf(l3_moe_w_gate), bf(l3_moe_w_up), bf(l3_moe_w_down))
    b_in0 = l0_conv_b_in.reshape(1, -1)
    b_in2 = l2_rec_b_in.reshape(1, -1)

    def trunk(x3):
        batch, seq, _ = x3.shape
        x = x3.reshape(batch * seq, d)
        glu = _mm_pair(x, w_in0, b_in0, mode="glu")
        x = _conv_out_ln(glu, x, l0_conv_w_dw, l0_conv_b_dw, l0_conv_norm_g, l0_conv_norm_b,
                         w_out0, l0_conv_b_out, l0_ln1_g, l0_ln1_b, seq)
        x = _ffn_ln(x, *ffn0, l0_ln2_g, l0_ln2_b)
        qkv = _mm(x, w_qkv1, BF16)
        o = _diff_attn(qkv, diff_scalars, l1_attn_subln_g, batch, seq, n_diff, 1.0 - lambda_init)
        x = _proj_ln([o], x, w_out1, None, l1_ln1_g, l1_ln1_b)
        x = _moe_ln(x, l1_moe_w_router, *moe1, l1_ln2_g, l1_ln2_b)
        y, r = _mm_pair(x, w_in2, b_in2, mode="gelu")
        r = r.reshape(batch, seq, lru_w)
        hs = [_lru_dir(r, l2_rec_conv_w, l2_rec_conv_b, gate_w[dr], gate_b[dr], lru_c[dr:dr + 1],
                       reverse=bool(dr), left=lru_left).reshape(batch * seq, lru_w) for dr in range(2)]
        x = _proj_ln([hs[0], hs[1], y], x, w_out2, l2_rec_b_out, l2_ln1_g, l2_ln1_b, mode="lru")
        x = _ffn_ln(x, *ffn2, l2_ln2_g, l2_ln2_b)
        qkv = _mm(x, w_qkv3, BF16)
        o = _win_attn(qkv, swa_slopes, swa_sinks, batch, seq, n_swa)
        x = _proj_ln([o], x, w_out3, None, l3_ln1_g, l3_ln1_b)
        x = _moe_ln(x, l3_moe_w_router, *moe3, l3_ln2_g, l3_ln2_b)
        return x.reshape(batch, seq, d)

    return (trunk(x_prompt), trunk(x_sample))
```

```python
import functools
import math

import jax
import jax.numpy as jnp
from jax import lax
from jax.experimental import pallas as pl
from jax.experimental.pallas import tpu as pltpu

F32 = jnp.float32
BF16 = jnp.bfloat16
I32 = jnp.int32

N_LAYERS = 4
LN_EPS = 1e-5
SUBLN_EPS = 1e-5
ALPHA = (2 * N_LAYERS) ** 0.25
LRU_C = 8.0
DIFF_HEAD_DIM = 128
SWA_HEAD_DIM = 128
SWA_KV_HEADS = 4
SWA_BLOCK = 128
TOP_K = 2
LOG2E = math.log2(math.e)
LANES = 128
HALO_ROWS = 16
LRU_HALO = 8
MIB = 1024 * 1024


def _cparams(vmem_mib, n_axes):
    return pltpu.CompilerParams(
        dimension_semantics=("arbitrary",) * n_axes, vmem_limit_bytes=vmem_mib * MIB)


def _tile(n, pref):
    t = min(n, pref)
    while n % t:
        t //= 2
    return t


def _sigmoid(x):
    return 0.5 * jnp.tanh(0.5 * x) + 0.5


def _silu(x):
    return x * _sigmoid(x)


def _gelu_tanh(x):
    return 0.5 * x * (1.0 + jnp.tanh(math.sqrt(2.0 / math.pi) * (x + 0.044715 * (x * x * x))))


def _dot(a, b):
    return jnp.dot(a, b, preferred_element_type=F32)


def _dot_nt(a, b):
    return lax.dot_general(a, b, (((1,), (1,)), ((), ())), preferred_element_type=F32)


def _layer_norm(y, g, b):
    mu = jnp.mean(y, axis=-1, keepdims=True)
    yc = y - mu
    var = jnp.mean(yc * yc, axis=-1, keepdims=True)
    return yc * lax.rsqrt(var + LN_EPS) * g + b


def _mm_body(x_ref, w_ref, o_ref, xb):
    @pl.when(pl.program_id(1) == 0)
    def _():
        xb[...] = x_ref[...].astype(BF16)

    o_ref[...] = _dot(xb[...], w_ref[...]).astype(o_ref.dtype)


def _mm(x, w, out_dtype):
    m, k = x.shape
    n = w.shape[1]
    tm, tn = _tile(m, 1024), _tile(n, 2048)
    return pl.pallas_call(
        _mm_body,
        grid=(m // tm, n // tn),
        in_specs=[pl.BlockSpec((tm, k), lambda i, j: (i, 0)),
                  pl.BlockSpec((k, tn), lambda i, j: (0, j))],
        out_specs=pl.BlockSpec((tm, tn), lambda i, j: (i, j)),
        out_shape=jax.ShapeDtypeStruct((m, n), out_dtype),
        scratch_shapes=[pltpu.VMEM((tm, k), BF16)],
        compiler_params=_cparams(56, 2),
        name="mm",
    )(x, w)


def _mm_pair_body(x_ref, wa_ref, wg_ref, ba_ref, bg_ref, *rest, mode):
    xb = rest[-1]

    @pl.when(pl.program_id(1) == 0)
    def _():
        xb[...] = x_ref[...].astype(BF16)

    a = _dot(xb[...], wa_ref[...]) + ba_ref[...]
    g = _dot(xb[...], wg_ref[...]) + bg_ref[...]
    if mode == "glu":
        rest[0][...] = a * _sigmoid(g)
    else:
        rest[0][...] = _gelu_tanh(a).astype(rest[0].dtype)
        rest[1][...] = g


def _mm_pair(x, w, bias, *, mode):
    m, k = x.shape
    n = w.shape[1] // 2
    tm, tn = _tile(m, 1024), _tile(n, 1024)
    nj = n // tn
    out_spec = pl.BlockSpec((tm, tn), lambda i, j: (i, j))
    if mode == "glu":
        out_specs, out_shape = out_spec, jax.ShapeDtypeStruct((m, n), F32)
    else:
        out_specs = [out_spec, out_spec]
        out_shape = [jax.ShapeDtypeStruct((m, n), BF16), jax.ShapeDtypeStruct((m, n), F32)]
    return pl.pallas_call(
        functools.partial(_mm_pair_body, mode=mode),
        grid=(m // tm, nj),
        in_specs=[pl.BlockSpec((tm, k), lambda i, j: (i, 0)),
                  pl.BlockSpec((k, tn), lambda i, j: (0, j)),
                  pl.BlockSpec((k, tn), lambda i, j: (0, j + nj)),
                  pl.BlockSpec((1, tn), lambda i, j: (0, j)),
                  pl.BlockSpec((1, tn), lambda i, j: (0, j + nj))],
        out_specs=out_specs,
        out_shape=out_shape,
        scratch_shapes=[pltpu.VMEM((tm, k), BF16)],
        compiler_params=_cparams(56, 2),
        name="mm_" + mode,
    )(x, w, w, bias, bias)


def _conv_out_body(cur_ref, prv_ref, nxt_ref, res_ref, wdw_ref, bdw_ref, ng_ref, nb_ref,
                   w_ref, bo_ref, g_ref, b_ref, o_ref, ext, cslab, lhs, *, seq, ts, width):
    i = pl.program_id(0)
    t0 = i * ts
    first = (t0 % seq) == 0
    last = ((t0 + ts) % seq) == 0
    nsl = cur_ref.shape[1] // LANES
    half = (width - 1) // 2
    for c in range(nsl):
        cs = slice(c * LANES, (c + 1) * LANES)
        ext[c, 0:HALO_ROWS, :] = jnp.where(first, 0.0, prv_ref[:, cs])
        ext[c, HALO_ROWS:HALO_ROWS + ts, :] = cur_ref[:, cs]
        ext[c, HALO_ROWS + ts:, :] = jnp.where(last, 0.0, nxt_ref[:, cs])

    def slab(c, carry):
        wk = wdw_ref[c]
        acc = jnp.broadcast_to(bdw_ref[c], (ts, LANES))
        for k in range(width):
            acc = acc + wk[k:k + 1, :] * ext[c, pl.ds(HALO_ROWS - half + k, ts), :]
        cslab[c] = acc
        return carry

    lax.fori_loop(0, nsl, slab, 0)

    s1 = cslab[0]
    for c in range(1, nsl):
        s1 = s1 + cslab[c]
    mu = jnp.sum(s1, axis=-1, keepdims=True) * (1.0 / (nsl * LANES))
    s2 = jnp.square(cslab[0] - mu)
    for c in range(1, nsl):
        s2 = s2 + jnp.square(cslab[c] - mu)
    var = jnp.sum(s2, axis=-1, keepdims=True) * (1.0 / (nsl * LANES))
    rstd = lax.rsqrt(var + LN_EPS)
    for c in range(nsl):
        cs = slice(c * LANES, (c + 1) * LANES)
        v = (cslab[c] - mu) * rstd * ng_ref[:, cs] + nb_ref[:, cs]
        lhs[:, cs] = _silu(v).astype(BF16)
    h = _dot(lhs[...], w_ref[...]) + bo_ref[...]
    o_ref[...] = _layer_norm(ALPHA * res_ref[...] + h, g_ref[...], b_ref[...])


def _conv_out_ln(glu, res, w_dw, b_dw, norm_g, norm_b, w_out, b_out, ln_g, ln_b, seq):
    t, d = glu.shape
    width = w_dw.shape[0]
    assert (width - 1) // 2 <= HALO_ROWS and d % LANES == 0
    ts = _tile(seq, 256)
    nsl = d // LANES
    hb = ts // HALO_ROWS
    wdw = w_dw.reshape(width, nsl, LANES).transpose(1, 0, 2)
    bdw = b_dw.reshape(nsl, 1, LANES)
    row = lambda v: v.reshape(1, d)
    vec = pl.BlockSpec((1, d), lambda i: (0, 0))
    return pl.pallas_call(
        functools.partial(_conv_out_body, seq=seq, ts=ts, width=width),
        grid=(t // ts,),
        in_specs=[pl.BlockSpec((ts, d), lambda i: (i, 0)),
                  pl.BlockSpec((HALO_ROWS, d), lambda i: (jnp.maximum(i * hb - 1, 0), 0)),
                  pl.BlockSpec((HALO_ROWS, d), lambda i: (jnp.minimum((i + 1) * hb, t // HALO_ROWS - 1), 0)),
                  pl.BlockSpec((ts, d), lambda i: (i, 0)),
                  pl.BlockSpec((nsl, width, LANES), lambda i: (0, 0, 0)),
                  pl.BlockSpec((nsl, 1, LANES), lambda i: (0, 0, 0)),
                  vec, vec,
                  pl.BlockSpec((d, d), lambda i: (0, 0)),
                  vec, vec, vec],
        out_specs=pl.BlockSpec((ts, d), lambda i: (i, 0)),
        out_shape=jax.ShapeDtypeStruct((t, d), F32),
        scratch_shapes=[pltpu.VMEM((nsl, ts + 2 * HALO_ROWS, LANES), F32),
                        pltpu.VMEM((nsl, ts, LANES), F32),
                        pltpu.VMEM((ts, d), BF16)],
        compiler_params=_cparams(48, 1),
        name="conv_out_ln",
    )(glu, glu, glu, res, wdw, bdw, row(norm_g), row(norm_b), w_out, row(b_out), row(ln_g), row(ln_b))


def _proj_ln_body(*refs, mode, has_bias):
    if mode == "lru":
        hf_ref, hb_ref, y_ref = refs[:3]
        rest = refs[3:]
        lhs = ((hf_ref[...] + hb_ref[...]) * y_ref[...].astype(F32)).astype(BF16)
    else:
        rest = refs[1:]
        lhs = refs[0][...].astype(BF16)
    if has_bias:
        res_ref, w_ref, bo_ref, g_ref, b_ref, o_ref = rest
    else:
        res_ref, w_ref, g_ref, b_ref, o_ref = rest
    h = _dot(lhs, w_ref[...])
    if has_bias:
        h = h + bo_ref[...]
    o_ref[...] = _layer_norm(ALPHA * res_ref[...] + h, g_ref[...], b_ref[...])


def _proj_ln(lhs_args, res, w, b_out, ln_g, ln_b, *, mode="plain"):
    t, d = res.shape
    k = w.shape[0]
    tm = _tile(t, 512)
    tok = lambda n: pl.BlockSpec((tm, n), lambda i: (i, 0))
    vec = pl.BlockSpec((1, d), lambda i: (0, 0))
    row = lambda v: v.reshape(1, d)
    in_specs = [tok(k) for _ in lhs_args] + [tok(d), pl.BlockSpec((k, d), lambda i: (0, 0),
                                                                  pipeline_mode=pl.Buffered(1))]
    args = list(lhs_args) + [res, w]
    if b_out is not None:
        in_specs.append(vec)
        args.append(row(b_out))
    in_specs += [vec, vec]
    args += [row(ln_g), row(ln_b)]
    return pl.pallas_call(
        functools.partial(_proj_ln_body, mode=mode, has_bias=b_out is not None),
        grid=(t // tm,),
        in_specs=in_specs,
        out_specs=tok(d),
        out_shape=jax.ShapeDtypeStruct((t, d), F32),
        compiler_params=_cparams(56, 1),
        name="proj_ln_" + mode,
    )(*args)


def _ffn_body(x_ref, wg_ref, wu_ref, wd_ref, g_ref, b_ref, o_ref, xb, acc):
    f = pl.program_id(1)

    @pl.when(f == 0)
    def _():
        xb[...] = x_ref[...].astype(BF16)
        acc[...] = jnp.zeros_like(acc)

    g = _dot(xb[...], wg_ref[...])
    u = _dot(xb[...], wu_ref[...])
    acc[...] += _dot((_silu(g) * u).astype(BF16), wd_ref[...])

    @pl.when(f == pl.num_programs(1) - 1)
    def _():
        o_ref[...] = _layer_norm(ALPHA * x_ref[...] + acc[...], g_ref[...], b_ref[...])


def _ffn_ln(x, w_gate, w_up, w_down, ln_g, ln_b):
    t, d = x.shape
    fdim = w_gate.shape[1]
    tm, tf = _tile(t, 512), _tile(fdim, 1024)
    vec = pl.BlockSpec((1, d), lambda i, f: (0, 0))
    return pl.pallas_call(
        _ffn_body,
        grid=(t // tm, fdim // tf),
        in_specs=[pl.BlockSpec((tm, d), lambda i, f: (i, 0)),
                  pl.BlockSpec((d, tf), lambda i, f: (0, f)),
                  pl.BlockSpec((d, tf), lambda i, f: (0, f)),
                  pl.BlockSpec((tf, d), lambda i, f: (f, 0)),
                  vec, vec],
        out_specs=pl.BlockSpec((tm, d), lambda i, f: (i, 0)),
        out_shape=jax.ShapeDtypeStruct((t, d), F32),
        scratch_shapes=[pltpu.VMEM((tm, d), BF16), pltpu.VMEM((tm, d), F32)],
        compiler_params=_cparams(56, 2),
        name="ffn_ln",
    )(x, w_gate, w_up, w_down, ln_g.reshape(1, d), ln_b.reshape(1, d))


DIFF_KEY_RADIX = 256
DIFF_SC_PER_HEAD = 7


def _split3_f32(x):
    hi = x.astype(BF16).astype(F32)
    mid = (x - hi).astype(BF16).astype(F32)
    lo = (x - hi - mid).astype(BF16).astype(F32)
    return hi, mid, lo


def _lane_tiles(x, op):
    return functools.reduce(op, [x[:, t * LANES:(t + 1) * LANES] for t in range(x.shape[1] // LANES)])


def _diff_attn_body(sc_ref, q_ref, k_ref, v_ref, kx_ref, sg_ref, o_ref, qa, sfull, oacc, m128, l128,
                    *, seq, tq, nqt, out_scale):
    hd = DIFF_HEAD_DIM
    tk = tq
    nk = seq // tk
    h = pl.program_id(1)
    lam = sc_ref[0]
    base = 1 + DIFF_SC_PER_HEAD * h
    slope2 = sc_ref[base + 6]

    irel = lax.broadcasted_iota(I32, (tq, 1), 0).astype(F32)
    lane = lax.broadcasted_iota(I32, (tq, hd), 1)
    ext = jnp.zeros((tq, hd), F32)
    for j, v in enumerate((*_split3_f32(-slope2 * irel), *[sc_ref[base + t] for t in range(6)])):
        ext = jnp.where(lane == j, v, ext)
    for qt in range(nqt):
        for c in range(2):
            qc = q_ref[qt * tq:(qt + 1) * tq, c * hd:(c + 1) * hd]
            qa[qt, c, 0] = jnp.concatenate([qc, ext.astype(BF16)], axis=1)
            qa[qt, c, 1] = jnp.concatenate([qc, (-ext).astype(BF16)], axis=1)
    d = (lax.broadcasted_iota(I32, (tq, tk), 0) - lax.broadcasted_iota(I32, (tq, tk), 1)).astype(F32)
    fix = (2.0 * slope2) * jnp.minimum(d, 0.0)
    m128[...] = jnp.full_like(m128, -jnp.inf)
    l128[...] = jnp.zeros_like(l128)
    oacc[...] = jnp.zeros_like(oacc)

    def tile(qt, r):
        iq = pl.program_id(2) * nqt + qt
        kt = lax.rem(iq + r, nk)
        j0 = pl.multiple_of(kt * tk, tk)
        if r == 0:
            return j0, 0, 0.0
        c0 = ((iq - kt) * tq).astype(F32)
        right = kt > iq
        return j0, right.astype(I32), jnp.where(right, slope2 * c0, -slope2 * c0)

    def scores(qt, r):
        j0, sgn, kappa = tile(qt, r)
        kk = k_ref[pl.ds(j0, tk), :]
        for c in range(2):
            ka = jnp.concatenate([kk[:, c * hd:(c + 1) * hd], kx_ref[...]], axis=1)
            s = _dot_nt(qa[qt, c, sgn], ka)
            if r == 0:
                s = s + fix
            sfull[qt, c, :, r * tk:(r + 1) * tk] = s
            m128[qt, c] = jnp.maximum(m128[qt, c], _lane_tiles(s, jnp.maximum) + kappa)

    def values(qt, r, mrow):
        j0, _, kappa = tile(qt, r)
        vv = v_ref[pl.ds(j0, tk), :]
        for c in range(2):
            p = jnp.exp2(sfull[qt, c, :, r * tk:(r + 1) * tk] - (mrow[c] - kappa))
            l128[qt, c] += _lane_tiles(p, jnp.add)
            oacc[qt, c] += _dot(p.astype(BF16), vv)

    def row_max(qt):
        return [jnp.max(m128[qt, c], axis=-1, keepdims=True) for c in range(2)]

    def finish(qt):
        lsum = [jnp.sum(l128[qt, c], axis=-1, keepdims=True) for c in range(2)]
        o = oacc[qt, 0] / lsum[0] - lam * (oacc[qt, 1] / lsum[1])
        o = o * lax.rsqrt(jnp.mean(o * o, axis=-1, keepdims=True) + SUBLN_EPS) * (sg_ref[...] * out_scale)
        o_ref[qt * tq:(qt + 1) * tq, :] = o.astype(o_ref.dtype)

    for r in range(nk):
        scores(0, r)
    for qt in range(nqt):
        mrow = row_max(qt)
        for r in range(nk):
            if qt + 1 < nqt:
                scores(qt + 1, r)
            values(qt, r, mrow)
        finish(qt)


def _diff_attn(qkv, scalars, subln_g, batch, seq, n_heads, out_scale):
    t = qkv.shape[0]
    hd = DIFF_HEAD_DIM
    vd = 2 * hd
    tq = _tile(seq, 512)
    nqt = 2 if (seq // tq) % 2 == 0 else 1
    assert tq <= DIFF_KEY_RADIX * DIFF_KEY_RADIX
    nq = seq // (tq * nqt)
    lane = jnp.arange(hd)[None, :]
    jrel = jnp.arange(tq)[:, None]
    jl = (jrel % DIFF_KEY_RADIX).astype(F32)
    jh = (jrel // DIFF_KEY_RADIX).astype(F32)
    kx = jnp.where(lane < 3, 1.0, jnp.where(lane < 6, jl, jnp.where(lane < 9, jh, 0.0))).astype(BF16)
    return pl.pallas_call(
        functools.partial(_diff_attn_body, seq=seq, tq=tq, nqt=nqt, out_scale=out_scale),
        grid=(batch, n_heads, nq),
        in_specs=[pl.BlockSpec(memory_space=pltpu.SMEM),
                  pl.BlockSpec((nqt * tq, vd), lambda b, h, i: (b * nq + i, h)),
                  pl.BlockSpec((seq, vd), lambda b, h, i: (b, n_heads + h)),
                  pl.BlockSpec((seq, vd), lambda b, h, i: (b, 2 * n_heads + h)),
                  pl.BlockSpec((tq, hd), lambda b, h, i: (0, 0)),
                  pl.BlockSpec((1, vd), lambda b, h, i: (0, 0))],
        out_specs=pl.BlockSpec((nqt * tq, vd), lambda b, h, i: (b * nq + i, h)),
        out_shape=jax.ShapeDtypeStruct((t, n_heads * vd), BF16),
        scratch_shapes=[pltpu.VMEM((nqt, 2, 2, tq, vd), BF16), pltpu.VMEM((nqt, 2, tq, seq), F32),
                        pltpu.VMEM((nqt, 2, tq, vd), F32), pltpu.VMEM((nqt, 2, tq, LANES), F32),
                        pltpu.VMEM((nqt, 2, tq, LANES), F32)],
        compiler_params=_cparams(58, 3),
        name="diff_attn",
    )(scalars, qkv, qkv, qkv, kx, subln_g.reshape(1, vd))


def _win_attn_body(sc_ref, bias_ref, q_ref, kp_ref, kc_ref, kn_ref, vp_ref, vc_ref, vn_ref, o_ref,
                   *, n_heads):
    d, blk = SWA_HEAD_DIM, SWA_BLOCK
    grp = n_heads // SWA_KV_HEADS
    rows = grp * blk
    head_of_row = lax.broadcasted_iota(I32, (rows, 1), 0) // blk
    for kv in range(SWA_KV_HEADS):
        cs = slice(kv * d, (kv + 1) * d)
        kw = jnp.concatenate([kp_ref[:, cs], kc_ref[:, cs], kn_ref[:, cs]], axis=0)
        vw = jnp.concatenate([vp_ref[:, cs], vc_ref[:, cs], vn_ref[:, cs]], axis=0)
        qg = jnp.concatenate([q_ref[:, (kv * grp + g) * d:(kv * grp + g + 1) * d] for g in range(grp)], axis=0)
        sink = jnp.zeros((rows, 1), F32)
        for g in range(grp):
            sink = jnp.where(head_of_row == g, sc_ref[kv * grp + g], sink)
        s = _dot_nt(qg, kw) - bias_ref[0, kv]
        m = jnp.maximum(jnp.max(s, axis=-1, keepdims=True), sink)
        p = jnp.exp2(s - m)
        denom = jnp.sum(p, axis=-1, keepdims=True) + jnp.exp2(sink - m)
        o = _dot(p.astype(BF16), vw) / denom
        for g in range(grp):
            hcol = (kv * grp + g) * d
            o_ref[:, hcol:hcol + d] = o[g * blk:(g + 1) * blk, :].astype(o_ref.dtype)


def _win_attn(qkv, slopes, sinks, batch, seq, n_heads):
    t = qkv.shape[0]
    d, blk = SWA_HEAD_DIM, SWA_BLOCK
    nb = seq // blk
    qw, kvw = n_heads * d, SWA_KV_HEADS * d
    assert qw % kvw == 0 and nb >= 2
    grp = n_heads // SWA_KV_HEADS
    rows = grp * blk
    qrow = jnp.arange(rows)[:, None] % blk
    col = jnp.arange(3 * blk)[None, :]
    rel = jnp.abs(qrow - col + blk)
    band = rel <= blk
    valid = jnp.stack([band & (col >= blk), band, band & (col < 2 * blk)])
    slope_rows = jnp.repeat(slopes.reshape(SWA_KV_HEADS, grp), blk, axis=1)
    bias = jnp.where(valid[:, None], slope_rows[None, :, :, None] * rel.astype(F32)[None, None], jnp.inf)
    edge = lambda n: jnp.where(n == 0, 0, jnp.where(n == nb - 1, 2, 1))
    kcol, vcol = qw // kvw, qw // kvw + 1
    prev = lambda b, n: b * nb + jnp.maximum(n - 1, 0)
    nxt = lambda b, n: b * nb + jnp.minimum(n + 1, nb - 1)
    cur = lambda b, n: b * nb + n
    kvspec = lambda f, col: pl.BlockSpec((blk, kvw), lambda b, n: (f(b, n), col))
    return pl.pallas_call(
        functools.partial(_win_attn_body, n_heads=n_heads),
        grid=(batch, nb),
        in_specs=[pl.BlockSpec(memory_space=pltpu.SMEM),
                  pl.BlockSpec((1, SWA_KV_HEADS, rows, 3 * blk), lambda b, n: (edge(n), 0, 0, 0)),
                  pl.BlockSpec((blk, qw), lambda b, n: (cur(b, n), 0)),
                  kvspec(prev, kcol), kvspec(cur, kcol), kvspec(nxt, kcol),
                  kvspec(prev, vcol), kvspec(cur, vcol), kvspec(nxt, vcol)],
        out_specs=pl.BlockSpec((blk, qw), lambda b, n: (cur(b, n), 0)),
        out_shape=jax.ShapeDtypeStruct((t, qw), BF16),
        compiler_params=_cparams(32, 2),
        name="win_attn",
    )(sinks, bias, qkv, qkv, qkv, qkv, qkv, qkv, qkv)


def _lru_body(r_hbm, cw_ref, cb_ref, gw_ref, gb_ref, c_ref, h_hbm,
              rbuf, hout, abuf, bbuf, hcar, isem, osem, *, seq, ts, batch, reverse, left):
    i = pl.program_id(0)
    nch = seq // ts
    taps = cw_ref.shape[0]
    right = taps - 1 - left
    hal = LRU_HALO
    width = r_hbm.shape[2]
    nblk, bw = gw_ref.shape[1], gw_ref.shape[2]
    chunk = lambda s: (nch - 1 - s) if reverse else s
    slot = i % 2

    def in_copies(ci, sl, op):
        t0 = ci * ts

        def each(src_start, dst_start, rows):
            src_start = pl.multiple_of(src_start, LRU_HALO)
            for b in range(batch):
                cp = pltpu.make_async_copy(r_hbm.at[b, pl.ds(src_start, rows), :],
                                           rbuf.at[sl, pl.ds(dst_start, rows), b, :], isem.at[sl])
                getattr(cp, op)()

        each(t0, hal, ts)
        pl.when(ci > 0)(lambda: each(jnp.maximum(t0 - hal, 0), 0, hal))
        pl.when(ci < nch - 1)(lambda: each(jnp.minimum(t0 + ts, seq - hal), hal + ts, hal))

    def out_copies(ci, sl, op):
        for b in range(batch):
            cp = pltpu.make_async_copy(hout.at[sl, pl.ds(0, ts), b, :],
                                       h_hbm.at[b, pl.ds(ci * ts, ts), :], osem.at[sl])
            getattr(cp, op)()

    @pl.when(i == 0)
    def _():
        hcar[...] = jnp.zeros_like(hcar)
        in_copies(chunk(0), 0, "start")

    @pl.when(i + 1 < nch)
    def _():
        in_copies(chunk(i + 1), 1 - slot, "start")

    ci = chunk(i)
    in_copies(ci, slot, "wait")

    @pl.when(ci == 0)
    def _():
        rbuf[slot, hal - left:hal] = jnp.zeros((left, batch, width), F32)

    @pl.when(ci == nch - 1)
    def _():
        rbuf[slot, hal + ts:hal + ts + right] = jnp.zeros((right, batch, width), F32)

    rf = jnp.broadcast_to(cb_ref[...], (ts, batch, width))
    for k in range(taps):
        rf = rf + cw_ref[k:k + 1, :] * rbuf[slot, pl.ds(hal - left + k, ts)]
    rf2 = rf.reshape(ts * batch, width)
    for nbk in range(nblk):
        cs = slice(nbk * bw, (nbk + 1) * bw)
        xb = rf2[:, cs]
        xbb = xb.astype(BF16)
        tr = jnp.tanh(_dot(xbb, gw_ref[0, nbk]) + gb_ref[0:1, cs])
        ti = jnp.tanh(_dot(xbb, gw_ref[1, nbk]) + gb_ref[1:2, cs])
        a = jnp.exp2(c_ref[:, cs] * tr + c_ref[:, cs])
        hx = 0.5 * xb
        bb = jnp.sqrt(1.0 - a * a) * (hx * ti + hx)
        abuf[:, :, cs] = a.reshape(ts, batch, bw)
        bbuf[:, :, cs] = bb.reshape(ts, batch, bw)

    @pl.when(i >= 2)
    def _():
        out_copies(ci, slot, "wait")

    def step(s, h):
        t = (ts - 1 - s) if reverse else s
        h = abuf[t] * h + bbuf[t]
        hout[slot, t] = h
        return h

    hcar[...] = lax.fori_loop(0, ts, step, hcar[...])
    out_copies(ci, slot, "start")

    @pl.when(i == nch - 1)
    def _():
        out_copies(ci, slot, "wait")
        if nch >= 2:
            out_copies(ci, 1 - slot, "wait")


def _lru_dir(r3, conv_w, conv_b, gate_w, gate_b, cvec, *, reverse, left):
    batch, seq, width = r3.shape
    ts = _tile(seq, 512 // batch)
    taps = conv_w.shape[0]
    assert batch % 8 == 0 and ts % LRU_HALO == 0 and max(left, taps - 1 - left) <= LRU_HALO
    full = lambda a: pl.BlockSpec(a.shape, lambda i: (0,) * a.ndim)
    cb = conv_b.reshape(1, width)
    return pl.pallas_call(
        functools.partial(_lru_body, seq=seq, ts=ts, batch=batch, reverse=reverse, left=left),
        grid=(seq // ts,),
        in_specs=[pl.BlockSpec(memory_space=pl.ANY), full(conv_w), full(cb), full(gate_w),
                  full(gate_b), full(cvec)],
        out_specs=pl.BlockSpec(memory_space=pl.ANY),
        out_shape=jax.ShapeDtypeStruct((batch, seq, width), F32),
        scratch_shapes=[pltpu.VMEM((2, ts + 2 * LRU_HALO, batch, width), F32),
                        pltpu.VMEM((2, ts, batch, width), F32),
                        pltpu.VMEM((ts, batch, width), F32),
                        pltpu.VMEM((ts, batch, width), F32),
                        pltpu.VMEM((batch, width), F32),
                        pltpu.SemaphoreType.DMA((2,)),
                        pltpu.SemaphoreType.DMA((2,))],
        compiler_params=_cparams(48, 1),
        name="lru_rev" if reverse else "lru_fwd",
    )(r3, conv_w, cb, gate_w, gate_b, cvec)


def _split_bf16(x):
    hi = x.astype(BF16)
    lo = (x - hi.astype(F32)).astype(BF16)
    return hi, lo


def _router_body(x_ref, wr_ref, o_ref, cnt_ref, cnt, lower):
    i = pl.program_id(0)
    tm = x_ref.shape[0]
    ne = wr_ref.shape[1]

    @pl.when(i == 0)
    def _():
        cnt[...] = jnp.zeros_like(cnt)
        lower[...] = jnp.where(lax.broadcasted_iota(I32, (tm, tm), 1) < lax.broadcasted_iota(I32, (tm, tm), 0),
                               1.0, 0.0).astype(BF16)

    xh, xl = _split_bf16(x_ref[...])
    wh, wl = _split_bf16(wr_ref[...])
    logits = _dot(xh, wh) + _dot(xl, wh) + _dot(xh, wl)
    lane = lax.broadcasted_iota(I32, (tm, ne), 1).astype(F32)
    v1 = jnp.max(logits, axis=-1, keepdims=True)
    i1 = jnp.min(jnp.where(logits == v1, lane, float(ne)), axis=-1, keepdims=True)
    rest = jnp.where(lane == i1, -jnp.inf, logits)
    v2 = jnp.max(rest, axis=-1, keepdims=True)
    i2 = jnp.min(jnp.where(rest == v2, lane, float(ne)), axis=-1, keepdims=True)
    e2 = jnp.exp(v2 - v1)
    w1 = 1.0 / (1.0 + e2)
    w2 = e2 / (1.0 + e2)
    oh1 = lane == i1
    oh2 = lane == i2
    oh = jnp.where(oh1 | oh2, 1.0, 0.0)
    before = _dot(lower[...], oh.astype(BF16)) + cnt[...]
    r1 = jnp.sum(jnp.where(oh1, before, 0.0), axis=-1, keepdims=True)
    r2 = jnp.sum(jnp.where(oh2, before, 0.0), axis=-1, keepdims=True)
    cnt[...] += jnp.sum(oh, axis=0, keepdims=True)
    cnt_ref[...] = cnt[...]
    col = lax.broadcasted_iota(I32, (tm, 8), 1)
    out = jnp.zeros((tm, 8), F32)
    for j, v in enumerate((i1, i2, w1, w2, r1, r2)):
        out = jnp.where(col == j, v, out)
    o_ref[...] = out


def _router(x, w_router):
    t, d = x.shape
    ne = w_router.shape[1]
    tm = _tile(t, 512)
    return pl.pallas_call(
        _router_body,
        grid=(t // tm,),
        in_specs=[pl.BlockSpec((tm, d), lambda i: (i, 0)),
                  pl.BlockSpec((d, ne), lambda i: (0, 0))],
        out_specs=[pl.BlockSpec((tm, 8), lambda i: (i, 0)),
                   pl.BlockSpec((1, ne), lambda i: (0, 0))],
        out_shape=[jax.ShapeDtypeStruct((t, 8), F32), jax.ShapeDtypeStruct((1, ne), F32)],
        scratch_shapes=[pltpu.VMEM((1, ne), F32), pltpu.VMEM((tm, tm), BF16)],
        compiler_params=_cparams(32, 1),
        name="router",
    )(x, w_router)


ROW_DMA_GROUP = 8


def _start_row_gather(idx_ref, n_rows, src_hbm, dst, sem):
    def group(g, carry):
        for j in range(ROW_DMA_GROUP):
            r = g * ROW_DMA_GROUP + j
            row = idx_ref[0, 0, r]
            pltpu.make_async_copy(src_hbm.at[pl.ds(row, 1), :], dst.at[pl.ds(r, 1), :], sem).start()
        return carry
    lax.fori_loop(0, n_rows // ROW_DMA_GROUP, group, 0)


def _wait_rows(n_rows, hbm, vmem, sem):
    pltpu.make_async_copy(hbm.at[pl.ds(0, n_rows), :], vmem, sem).wait()


def _dispatch_body(off_ref, pad_ref, nv_ref, pos_ref, x_ref, xs_hbm, zbuf, sem):
    i = pl.program_id(0)
    tm = x_ref.shape[0]
    tz = zbuf.shape[0]

    @pl.when(i == 0)
    def _():
        zbuf[...] = jnp.zeros_like(zbuf)

        def zero_tile(row0):
            row0 = pl.multiple_of(row0, tz)
            cp = pltpu.make_async_copy(zbuf, xs_hbm.at[pl.ds(row0, tz), :], sem.at[1])
            cp.start()
            cp.wait()

        for e in range(off_ref.shape[0]):
            pl.when(pad_ref[e] > 0)(functools.partial(zero_tile, off_ref[e] + pad_ref[e] - tz))

        def tail(j, carry):
            zero_tile(j * tz)
            return carry
        lax.fori_loop(nv_ref[0], xs_hbm.shape[0] // tz, tail, 0)

    def group(g, carry):
        for j in range(ROW_DMA_GROUP):
            r = g * ROW_DMA_GROUP + j
            for slot in range(TOP_K):
                dst = pos_ref[0, 0, slot * tm + r]
                pltpu.make_async_copy(x_ref.at[pl.ds(r, 1), :], xs_hbm.at[pl.ds(dst, 1), :], sem.at[0]).start()
        return carry

    lax.fori_loop(0, tm // ROW_DMA_GROUP, group, 0)
    for _ in range(TOP_K):
        _wait_rows(tm, xs_hbm, x_ref, sem.at[0])


def _dispatch(x, pos_tiles, offsets, padded, n_valid, n_rows, tm, tm_e):
    t, d = x.shape
    grid_spec = pltpu.PrefetchScalarGridSpec(
        num_scalar_prefetch=3,
        grid=(t // tm,),
        in_specs=[pl.BlockSpec((1, 1, TOP_K * tm), lambda i, o, p, nv: (i, 0, 0), memory_space=pltpu.SMEM),
                  pl.BlockSpec((tm, d), lambda i, o, p, nv: (i, 0))],
        out_specs=pl.BlockSpec(memory_space=pl.ANY),
        scratch_shapes=[pltpu.VMEM((tm_e, d), F32), pltpu.SemaphoreType.DMA((2,))],
    )
    return pl.pallas_call(
        _dispatch_body,
        grid_spec=grid_spec,
        out_shape=jax.ShapeDtypeStruct((n_rows, d), F32),
        compiler_params=_cparams(32, 1),
        name="moe_dispatch",
    )(offsets, padded, n_valid, pos_tiles, x)


def _expert_body(te_ref, nv_ref, x_ref, wg_ref, wu_ref, wd_ref, o_ref):
    j = pl.program_id(0)

    @pl.when(j < nv_ref[0])
    def _():
        xb = x_ref[...].astype(BF16)
        g = _dot(xb, wg_ref[0])
        u = _dot(xb, wu_ref[0])
        o_ref[...] = _dot((_silu(g) * u).astype(BF16), wd_ref[0])

    @pl.when(j >= nv_ref[0])
    def _():
        o_ref[...] = jnp.zeros_like(o_ref)


def _experts(xs, tile_expert, n_valid, w_gate, w_up, w_down, tm):
    n_rows, d = xs.shape
    fdim = w_gate.shape[2]
    nt = n_rows // tm
    wspec = lambda shape: pl.BlockSpec((1,) + shape, lambda j, te, nv: (te[j], 0, 0))
    grid_spec = pltpu.PrefetchScalarGridSpec(
        num_scalar_prefetch=2,
        grid=(nt,),
        in_specs=[pl.BlockSpec((tm, d), lambda j, te, nv: (jnp.minimum(j, nv[0] - 1), 0)),
                  wspec((d, fdim)), wspec((d, fdim)), wspec((fdim, d))],
        out_specs=pl.BlockSpec((tm, d), lambda j, te, nv: (j, 0)),
    )
    return pl.pallas_call(
        _expert_body,
        grid_spec=grid_spec,
        out_shape=jax.ShapeDtypeStruct((n_rows, d), F32),
        compiler_params=_cparams(56, 1),
        name="experts",
    )(tile_expert, n_valid, xs, w_gate, w_up, w_down)


def _combine_body(pos_cur, pos_nxt, x_ref, route_ref, g_ref, b_ref, ys_hbm, o_ref, buf, sem):
    i = pl.program_id(0)
    n = pl.num_programs(0)
    tm = x_ref.shape[0]
    slot = i % 2

    @pl.when(i == 0)
    def _():
        _start_row_gather(pos_cur, TOP_K * tm, ys_hbm, buf.at[0], sem.at[0])

    @pl.when(i + 1 < n)
    def _():
        _start_row_gather(pos_nxt, TOP_K * tm, ys_hbm, buf.at[1 - slot], sem.at[1 - slot])

    _wait_rows(TOP_K * tm, ys_hbm, buf.at[slot], sem.at[slot])
    f = route_ref[:, 2:3] * buf[slot, 0:tm, :] + route_ref[:, 3:4] * buf[slot, tm:2 * tm, :]
    o_ref[...] = _layer_norm(ALPHA * x_ref[...] + f, g_ref[...], b_ref[...])


def _combine_ln(x, ys, route, pos, ln_g, ln_b, tm):
    t, d = x.shape
    n = t // tm
    pos_spec = lambda f: pl.BlockSpec((1, 1, TOP_K * tm), lambda i: (f(i), 0, 0), memory_space=pltpu.SMEM)
    vec = pl.BlockSpec((1, d), lambda i: (0, 0))
    return pl.pallas_call(
        _combine_body,
        grid=(n,),
        in_specs=[pos_spec(lambda i: i), pos_spec(lambda i: jnp.minimum(i + 1, n - 1)),
                  pl.BlockSpec((tm, d), lambda i: (i, 0)),
                  pl.BlockSpec((tm, route.shape[1]), lambda i: (i, 0)), vec, vec,
                  pl.BlockSpec(memory_space=pl.ANY)],
        out_specs=pl.BlockSpec((tm, d), lambda i: (i, 0)),
        out_shape=jax.ShapeDtypeStruct((t, d), F32),
        scratch_shapes=[pltpu.VMEM((2, TOP_K * tm, d), F32), pltpu.SemaphoreType.DMA((2,))],
        compiler_params=_cparams(52, 1),
        name="moe_combine_ln",
    )(pos, pos, x, route, ln_g.reshape(1, d), ln_b.reshape(1, d), ys)


def _moe_ln(x, w_router, w_gate, w_up, w_down, ln_g, ln_b):
    t, d = x.shape
    ne = w_router.shape[1]
    tm_e = _tile(t, 512)
    tm_c = _tile(t, 512)
    route, counts = _router(x, w_router)
    expert = route[:, 0:2].astype(I32)
    rank = route[:, 4:6].astype(I32)
    counts = counts[0].astype(I32)
    padded = ((counts + tm_e - 1) // tm_e) * tm_e
    ends = jnp.cumsum(padded)
    offsets = ends - padded
    pos = offsets[expert] + rank
    nt = (TOP_K * t) // tm_e + ne
    tile_expert = jnp.minimum(
        jnp.searchsorted(ends, jnp.arange(nt, dtype=I32) * tm_e, side="right"), ne - 1).astype(I32)
    n_valid = (ends[-1] // tm_e).astype(I32).reshape(1)
    tiles = lambda tm: pos.reshape(t // tm, tm, TOP_K).transpose(0, 2, 1).reshape(t // tm, 1, TOP_K * tm)
    xs = _dispatch(x, tiles(tm_c), offsets, padded, n_valid, nt * tm_e, tm_c, tm_e)
    ys = _experts(xs, tile_expert, n_valid, w_gate, w_up, w_down, tm_e)
    return _combine_ln(x, ys, route, tiles(tm_c), ln_g, ln_b, tm_c)


def _alibi_slopes(n_heads):
    return 2.0 ** (-8.0 * jnp.arange(1, n_heads + 1, dtype=F32) / n_heads)


def kernel(x_prompt, x_sample, l0_conv_w_in, l0_conv_b_in, l0_conv_w_dw, l0_conv_b_dw, l0_conv_norm_g, l0_conv_norm_b, l0_conv_w_out, l0_conv_b_out, l0_ln1_g, l0_ln1_b, l0_ffn_w_gate, l0_ffn_w_up, l0_ffn_w_down, l0_ln2_g, l0_ln2_b, l1_attn_w_qkv, l1_attn_lambda, l1_attn_subln_g, l1_attn_w_out, l1_ln1_g, l1_ln1_b, l1_moe_w_router, l1_moe_w_gate, l1_moe_w_up, l1_moe_w_down, l1_ln2_g, l1_ln2_b, l2_rec_w_in, l2_rec_b_in, l2_rec_conv_w, l2_rec_conv_b, l2_rec_gate_w, l2_rec_gate_b, l2_rec_lambda, l2_rec_w_out, l2_rec_b_out, l2_ln1_g, l2_ln1_b, l2_ffn_w_gate, l2_ffn_w_up, l2_ffn_w_down, l2_ln2_g, l2_ln2_b, l3_attn_w_qkv, l3_attn_sink, l3_attn_w_out, l3_ln1_g, l3_ln1_b, l3_moe_w_router, l3_moe_w_gate, l3_moe_w_up, l3_moe_w_down, l3_ln2_g, l3_ln2_b):
    d = x_prompt.shape[-1]
    bf = lambda w: w.astype(BF16)

    diff_w = l1_attn_w_qkv.shape[1] // 3
    n_diff = diff_w // (2 * DIFF_HEAD_DIM)
    col_scale = jnp.concatenate([jnp.full((diff_w,), LOG2E * DIFF_HEAD_DIM ** -0.5, F32), jnp.ones((2 * diff_w,), F32)])
    w_qkv1 = bf(l1_attn_w_qkv * col_scale)
    lam = l1_attn_lambda.astype(F32)
    lambda_init = 0.8 - 0.6 * math.exp(-0.3 * 1)
    lam_full = jnp.exp(jnp.sum(lam[0] * lam[1])) - jnp.exp(jnp.sum(lam[2] * lam[3])) + lambda_init
    slope2 = LOG2E * _alibi_slopes(n_diff)
    diff_scalars = jnp.concatenate(
        [lam_full.reshape(1),
         jnp.stack([*_split3_f32(slope2), *_split3_f32(DIFF_KEY_RADIX * slope2), slope2], axis=1).reshape(-1)])

    lru_w = l2_rec_w_out.shape[0]
    lru_left = l2_rec_conv_w.shape[0] // 2
    lru_c = (-0.5 * LOG2E * LRU_C) * jax.nn.softplus(-l2_rec_lambda.astype(F32))
    gate_w = bf(0.5 * l2_rec_gate_w)
    gate_b = 0.5 * l2_rec_gate_b

    n_swa = d // SWA_HEAD_DIM
    swa_q = n_swa * SWA_HEAD_DIM
    col_scale3 = jnp.concatenate([jnp.full((swa_q,), LOG2E * SWA_HEAD_DIM ** -0.5, F32),
                                  jnp.ones((l3_attn_w_qkv.shape[1] - swa_q,), F32)])
    w_qkv3 = bf(l3_attn_w_qkv * col_scale3)
    swa_slopes = LOG2E * _alibi_slopes(n_swa)
    swa_sinks = LOG2E * l3_attn_sink.astype(F32)

    w_in0, w_out0 = bf(l0_conv_w_in), bf(l0_conv_w_out)
    ffn0 = (bf(l0_ffn_w_gate), bf(l0_ffn_w_up), bf(l0_ffn_w_down))
    w_out1 = bf(l1_attn_w_out)
    moe1 = (bf(l1_moe_w_gate), bf(l1_moe_w_up), bf(l1_moe_w_down))
    w_in2, w_out2 = bf(l2_rec_w_in), bf(l2_rec_w_out)
    ffn2 = (bf(l2_ffn_w_gate), bf(l2_ffn_w_up), bf(l2_ffn_w_down))
    w_out3 = bf(l3_attn_w_out)
    moe3 = (bf(l3_moe_w_gate), bf(l3_moe_w_up), bf(l3_moe_w_down))
    b_in0 = l0_conv_b_in.reshape(1, -1)
    b_in2 = l2_rec_b_in.reshape(1, -1)

    def trunk(x3):
        batch, seq, _ = x3.shape
        x = x3.reshape(batch * seq, d)
        glu = _mm_pair(x, w_in0, b_in0, mode="glu")
        x = _conv_out_ln(glu, x, l0_conv_w_dw, l0_conv_b_dw, l0_conv_norm_g, l0_conv_norm_b,
                         w_out0, l0_conv_b_out, l0_ln1_g, l0_ln1_b, seq)
        x = _ffn_ln(x, *ffn0, l0_ln2_g, l0_ln2_b)
        qkv = _mm(x, w_qkv1, BF16)
        o = _diff_attn(qkv, diff_scalars, l1_attn_subln_g, batch, seq, n_diff, 1.0 - lambda_init)
        x = _proj_ln([o], x, w_out1, None, l1_ln1_g, l1_ln1_b)
        x = _moe_ln(x, l1_moe_w_router, *moe1, l1_ln2_g, l1_ln2_b)
        y, r = _mm_pair(x, w_in2, b_in2, mode="gelu")
        r = r.reshape(batch, seq, lru_w)
        hs = [_lru_dir(r, l2_rec_conv_w, l2_rec_conv_b, gate_w[dr], gate_b[dr], lru_c[dr:dr + 1],
                       reverse=bool(dr), left=lru_left).reshape(batch * seq, lru_w) for dr in range(2)]
        x = _proj_ln([hs[0], hs[1], y], x, w_out2, l2_rec_b_out, l2_ln1_g, l2_ln1_b, mode="lru")
        x = _ffn_ln(x, *ffn2, l2_ln2_g, l2_ln2_b)
        qkv = _mm(x, w_qkv3, BF16)
        o = _win_attn(qkv, swa_slopes, swa_sinks, batch, seq, n_swa)
        x = _proj_ln([o], x, w_out3, None, l3_ln1_g, l3_ln1_b)
        x = _moe_ln(x, l3_moe_w_router, *moe3, l3_ln2_g, l3_ln2_b)
        return x.reshape(batch, seq, d)

    return (trunk(x_prompt), trunk(x_sample))
```
